```python
import jax, jax.numpy as jnp
from jax import lax
import numpy as np

D_MODEL = 1024
BATCH = 4
SEQ = 8192
DEPTH = 4
DEC_BATCH = 32
DEC_SEQ = 64
PAST_LEN = 4096

CHUNK = 64
D_MIX = D_MODEL
D_CONV = D_MIX // 4
D_ATT = D_MIX // 2
D_SC = D_MIX - D_CONV - D_ATT
N_HEADS_B = 8
HEAD_DIM_B = D_ATT // N_HEADS_B
CONV_A_WIDTH = 31
CONV_C_WIDTH = 3
BAND_CHUNKS = 8
BAND_PAST = BAND_CHUNKS * CHUNK
REL_CLIP = 128
N_REL = 2 * REL_CLIP + 1
N_MEM = 256
N_HEADS_X = 4
HEAD_DIM_X = D_MODEL // N_HEADS_X
D_FF = 4 * D_MODEL
EPS = 1e-6
NEG = -1e30
IN_COLS = 2 * D_CONV + 3 * D_ATT + 3 * D_SC
SPLITS = (D_CONV, 2 * D_CONV, 2 * D_CONV + D_ATT, 2 * D_CONV + 2 * D_ATT, 2 * D_CONV + 3 * D_ATT,
          2 * D_CONV + 3 * D_ATT + D_SC, 2 * D_CONV + 3 * D_ATT + 2 * D_SC)

kernel_name = 'hybrid_streaming_encoder_step'


def rmsnorm(x, g):
    xf = x.astype(jnp.float32)
    y = xf * lax.rsqrt(jnp.mean(xf * xf, axis=-1, keepdims=True) + EPS)
    return (y * g.astype(jnp.float32)).astype(x.dtype)


def layernorm(x, g, b):
    xf = x.astype(jnp.float32)
    mu = jnp.mean(xf, axis=-1, keepdims=True)
    var = jnp.mean(jnp.square(xf - mu), axis=-1, keepdims=True)
    y = (xf - mu) * lax.rsqrt(var + EPS) * g.astype(jnp.float32) + b.astype(jnp.float32)
    return y.astype(x.dtype)


def causal_dwconv(u_ext, w):
    return lax.conv_general_dilated(u_ext, w[:, None, :].astype(u_ext.dtype), window_strides=(1,),
                                    padding='VALID', dimension_numbers=('NWC', 'WIO', 'NWC'),
                                    feature_group_count=u_ext.shape[-1])


def chunk_band_attention(qc, kb, vb, key_valid, rel_bias, offset):
    nq, nk = qc.shape[2], kb.shape[2]
    rel = offset + jnp.arange(nq)[:, None] - jnp.arange(nk)[None, :]
    idx = jnp.clip(rel, -REL_CLIP, REL_CLIP) + REL_CLIP
    bias = rel_bias[:, idx].astype(jnp.float32)
    s = jnp.einsum('bnqhd,bnkhd->bnhqk', qc, kb).astype(jnp.float32) * (HEAD_DIM_B ** -0.5) + bias[None, None]
    s = jnp.where(key_valid[None, :, None, None, :], s, NEG)
    p = jax.nn.softmax(s, axis=-1).astype(vb.dtype)
    return jnp.einsum('bnhqk,bnkhd->bnqhd', p, vb)


def hybrid_mixer(xn, ctx_a, ctx_c, k_past, v_past, w_in, conv_a_w, conv_a_b, ln_a_g, ln_a_b,
                 q_norm, k_norm, rel_bias, conv_c_w, out_norm, w_out):
    bsz, t, _ = xn.shape
    z = xn @ w_in
    a_val, a_gate, q, k, v, g_b, g_c, h_c = jnp.split(z, SPLITS, axis=-1)
    u_a = a_val * jax.nn.sigmoid(a_gate)
    ua_ext = jnp.concatenate([ctx_a, u_a], axis=1)
    y_a = jax.nn.silu(layernorm(causal_dwconv(ua_ext, conv_a_w) + conv_a_b, ln_a_g, ln_a_b))
    new_a = ua_ext[:, -(CONV_A_WIDTH - 1):]
    uc_ext = jnp.concatenate([ctx_c, g_c * h_c], axis=1)
    y_c = g_b * causal_dwconv(uc_ext, conv_c_w)
    new_c = uc_ext[:, -(CONV_C_WIDTH - 1):]
    q = rmsnorm(q.reshape(bsz, t, N_HEADS_B, HEAD_DIM_B), q_norm)
    k = rmsnorm(k.reshape(bsz, t, N_HEADS_B, HEAD_DIM_B), k_norm)
    v = v.reshape(bsz, t, N_HEADS_B, HEAD_DIM_B)
    if k_past is None:
        nc = t // CHUNK
        qc = q.reshape(bsz, nc, CHUNK, N_HEADS_B, HEAD_DIM_B)
        pad = jnp.zeros((bsz, BAND_PAST, N_HEADS_B, HEAD_DIM_B), k.dtype)
        kp = jnp.concatenate([pad, k], axis=1).reshape(bsz, nc + BAND_CHUNKS, CHUNK, N_HEADS_B, HEAD_DIM_B)
        vp = jnp.concatenate([pad, v], axis=1).reshape(bsz, nc + BAND_CHUNKS, CHUNK, N_HEADS_B, HEAD_DIM_B)
        band_idx = jnp.arange(nc)[:, None] + jnp.arange(BAND_CHUNKS + 1)[None, :]
        kb = kp[:, band_idx].reshape(bsz, nc, (BAND_CHUNKS + 1) * CHUNK, N_HEADS_B, HEAD_DIM_B)
        vb = vp[:, band_idx].reshape(bsz, nc, (BAND_CHUNKS + 1) * CHUNK, N_HEADS_B, HEAD_DIM_B)
        valid = jnp.repeat(band_idx >= BAND_CHUNKS, CHUNK, axis=1)
        o = chunk_band_attention(qc, kb, vb, valid, rel_bias, BAND_PAST).reshape(bsz, t, D_ATT)
        keep = min(BAND_PAST, t)
        new_k, new_v = k[:, -keep:], v[:, -keep:]
    else:
        past = k_past.shape[1]
        kb = jnp.concatenate([k_past, k], axis=1)
        vb = jnp.concatenate([v_past, v], axis=1)
        valid = jnp.ones((1, past + t), dtype=bool)
        o = chunk_band_attention(q[:, None], kb[:, None], vb[:, None], valid, rel_bias, past).reshape(bsz, t, D_ATT)
        keep = min(BAND_PAST, past + t)
        new_k, new_v = kb[:, -keep:], vb[:, -keep:]
    y = jnp.concatenate([rmsnorm(y_a, out_norm[:D_CONV]),
                         rmsnorm(o, out_norm[D_CONV:D_CONV + D_ATT]),
                         rmsnorm(y_c, out_norm[D_CONV + D_ATT:])], axis=-1)
    return y @ w_out, new_a, new_k, new_v, new_c


def memory_kv(mem, norm_mem, w_xk, w_xv, xk_norm):
    bsz = mem.shape[0]
    mn = rmsnorm(mem, norm_mem)
    mk = rmsnorm((mn @ w_xk).reshape(bsz, N_MEM, N_HEADS_X, HEAD_DIM_X), xk_norm)
    mv = (mn @ w_xv).reshape(bsz, N_MEM, N_HEADS_X, HEAD_DIM_X)
    return mk, mv


def cross_attention(xn, mk, mv, w_xq, xq_norm, w_xo):
    bsz, t, _ = xn.shape
    q = rmsnorm((xn @ w_xq).reshape(bsz, t, N_HEADS_X, HEAD_DIM_X), xq_norm)
    s = jnp.einsum('bqhd,bkhd->bhqk', q, mk).astype(jnp.float32) * (HEAD_DIM_X ** -0.5)
    p = jax.nn.softmax(s, axis=-1).astype(mv.dtype)
    o = jnp.einsum('bhqk,bkhd->bqhd', p, mv).reshape(bsz, t, N_HEADS_X * HEAD_DIM_X)
    return o @ w_xo


def sq_relu_mlp(xn, w_up, w_down):
    return jnp.square(jax.nn.relu(xn @ w_up)) @ w_down


def setup_inputs(seed: int = 0) -> dict:
    key = jax.random.key(seed)
    keys = list(jax.random.split(key, 48))

    def nrm(shape, scale):
        return scale * jax.random.normal(keys.pop(), shape, jnp.float32)

    kv_len = min(BAND_PAST, PAST_LEN)
    d = D_MODEL
    return {
        'x_prompt': nrm((BATCH, SEQ, d), 1.0),
        'x_sample': nrm((DEC_BATCH, DEC_SEQ, d), 1.0),
        'cache_conv_a': nrm((DEPTH, DEC_BATCH, CONV_A_WIDTH - 1, D_CONV), 0.5),
        'cache_band_k': nrm((DEPTH, DEC_BATCH, kv_len, N_HEADS_B, HEAD_DIM_B), 1.0),
        'cache_band_v': nrm((DEPTH, DEC_BATCH, kv_len, N_HEADS_B, HEAD_DIM_B), 1.0),
        'cache_conv_c': nrm((DEPTH, DEC_BATCH, CONV_C_WIDTH - 1, D_SC), 1.0),
        'cache_mem_k': nrm((DEPTH, DEC_BATCH, N_MEM, N_HEADS_X, HEAD_DIM_X), 1.0),
        'cache_mem_v': nrm((DEPTH, DEC_BATCH, N_MEM, N_HEADS_X, HEAD_DIM_X), 1.0),
        'mem_prompt': nrm((BATCH, N_MEM, d), 1.0),
        'norm_mix': 1.0 + nrm((DEPTH, d), 0.02),
        'w_in': nrm((DEPTH, d, IN_COLS), d ** -0.5),
        'conv_a_w': nrm((DEPTH, CONV_A_WIDTH, D_CONV), CONV_A_WIDTH ** -0.5),
        'conv_a_b': nrm((DEPTH, D_CONV), 0.02),
        'ln_a_g': 1.0 + nrm((DEPTH, D_CONV), 0.02),
        'ln_a_b': nrm((DEPTH, D_CONV), 0.02),
        'q_norm': 1.0 + nrm((DEPTH, HEAD_DIM_B), 0.02),
        'k_norm': 1.0 + nrm((DEPTH, HEAD_DIM_B), 0.02),
        'rel_bias': nrm((DEPTH, N_HEADS_B, N_REL), 0.1),
        'conv_c_w': nrm((DEPTH, CONV_C_WIDTH, D_SC), CONV_C_WIDTH ** -0.5),
        'out_norm': 1.0 + nrm((DEPTH, D_MIX), 0.02),
        'w_out': nrm((DEPTH, D_MIX, d), D_MIX ** -0.5),
        'norm_x': 1.0 + nrm((DEPTH, d), 0.02),
        'norm_mem': 1.0 + nrm((DEPTH, d), 0.02),
        'w_xq': nrm((DEPTH, d, N_HEADS_X * HEAD_DIM_X), d ** -0.5),
        'w_xk': nrm((DEPTH, d, N_HEADS_X * HEAD_DIM_X), d ** -0.5),
        'w_xv': nrm((DEPTH, d, N_HEADS_X * HEAD_DIM_X), d ** -0.5),
        'xq_norm': 1.0 + nrm((DEPTH, HEAD_DIM_X), 0.02),
        'xk_norm': 1.0 + nrm((DEPTH, HEAD_DIM_X), 0.02),
        'w_xo': nrm((DEPTH, N_HEADS_X * HEAD_DIM_X, d), (N_HEADS_X * HEAD_DIM_X) ** -0.5),
        'norm_ff': 1.0 + nrm((DEPTH, d), 0.02),
        'w_up': nrm((DEPTH, d, D_FF), d ** -0.5),
        'w_down': nrm((DEPTH, D_FF, d), D_FF ** -0.5),
    }


def reference(x_prompt, x_sample, cache_conv_a, cache_band_k, cache_band_v, cache_conv_c,
              cache_mem_k, cache_mem_v, mem_prompt, norm_mix, w_in, conv_a_w, conv_a_b, ln_a_g, ln_a_b,
              q_norm, k_norm, rel_bias, conv_c_w, out_norm, w_out, norm_x, norm_mem, w_xq, w_xk, w_xv,
              xq_norm, xk_norm, w_xo, norm_ff, w_up, w_down):
    yp, ys = x_prompt, x_sample
    bp = x_prompt.shape[0]
    ca_p, kb_p, vb_p, cc_p, mk_p, mv_p = [], [], [], [], [], []
    ca_s, kb_s, vb_s, cc_s = [], [], [], []
    for l in range(DEPTH):
        mix_w = (w_in[l], conv_a_w[l], conv_a_b[l], ln_a_g[l], ln_a_b[l], q_norm[l], k_norm[l],
                 rel_bias[l], conv_c_w[l], out_norm[l], w_out[l])
        ctx_a0 = jnp.zeros((bp, CONV_A_WIDTH - 1, D_CONV), yp.dtype)
        ctx_c0 = jnp.zeros((bp, CONV_C_WIDTH - 1, D_SC), yp.dtype)
        h, sa, sk, sv, sc = hybrid_mixer(rmsnorm(yp, norm_mix[l]), ctx_a0, ctx_c0, None, None, *mix_w)
        yp = yp + h
        mk, mv = memory_kv(mem_prompt, norm_mem[l], w_xk[l], w_xv[l], xk_norm[l])
        yp = yp + cross_attention(rmsnorm(yp, norm_x[l]), mk, mv, w_xq[l], xq_norm[l], w_xo[l])
        yp = yp + sq_relu_mlp(rmsnorm(yp, norm_ff[l]), w_up[l], w_down[l])
        ca_p.append(sa); kb_p.append(sk); vb_p.append(sv); cc_p.append(sc); mk_p.append(mk); mv_p.append(mv)
        h, sa, sk, sv, sc = hybrid_mixer(rmsnorm(ys, norm_mix[l]), cache_conv_a[l], cache_conv_c[l],
                                         cache_band_k[l], cache_band_v[l], *mix_w)
        ys = ys + h
        ys = ys + cross_attention(rmsnorm(ys, norm_x[l]), cache_mem_k[l], cache_mem_v[l], w_xq[l], xq_norm[l], w_xo[l])
        ys = ys + sq_relu_mlp(rmsnorm(ys, norm_ff[l]), w_up[l], w_down[l])
        ca_s.append(sa); kb_s.append(sk); vb_s.append(sv); cc_s.append(sc)
    return (yp, ys,
            jnp.stack(ca_p), jnp.stack(kb_p), jnp.stack(vb_p), jnp.stack(cc_p), jnp.stack(mk_p), jnp.stack(mv_p),
            jnp.stack(ca_s), jnp.stack(kb_s), jnp.stack(vb_s), jnp.stack(cc_s))
```

```python
import functools

import jax
import jax.numpy as jnp
from jax import lax
from jax.experimental import pallas as pl
from jax.experimental.pallas import tpu as pltpu

EPS = 1e-6
NEG = -1e30
CHUNK = 64
BAND_CHUNKS = 8
BAND_PAST = BAND_CHUNKS * CHUNK
REL_CLIP = 128
N_HEADS_B = 8
HEAD_DIM_B = 64
N_HEADS_X = 4
HEAD_DIM_X = 256
CONV_A_WIDTH = 31
CONV_C_WIDTH = 3

SUBLANES = 8
LANES = 128
PAD_A = 32
PAD_C = 8
CONV_ROWS = 64
VMEM_LIMIT = 56 * 1024 * 1024

F32 = jnp.float32
BF16 = jnp.bfloat16


def _rms(x, g):
    return x * lax.rsqrt(jnp.mean(x * x, axis=-1, keepdims=True) + EPS) * g


def _dot(a, b):
    return jnp.dot(a, b, preferred_element_type=F32)


def _dot_nt(a, b):
    return lax.dot_general(a, b, (((1,), (1,)), ((), ())), preferred_element_type=F32)


def _mixer_in_kernel(x_ref, ctxa_ref, ctxc_ref, nmix_ref, win_ref, caw_ref, cab_ref, lng_ref,
                     lnb_ref, qn_ref, kn_ref, ccw_ref, ona_ref, onc_ref, hsum_ref,
                     ya_ref, yc_ref, q_ref, k_ref, v_ref, newa_ref, newc_ref, kf_ref, vf_ref,
                     ua_scr, uc_scr, *, bb, ts, d_conv, d_att, d_sc):
    t = pl.program_id(1)
    m = bb * ts
    ctx_a = CONV_A_WIDTH - 1
    ctx_c = CONV_C_WIDTH - 1

    @pl.when(t == 0)
    def _():
        ua_scr[:, PAD_A - ctx_a:PAD_A, :] = ctxa_ref[...]
        uc_scr[:, PAD_C - ctx_c:PAD_C, :] = ctxc_ref[...]

    @pl.when(t > 0)
    def _():
        ua_scr[:, PAD_A - ctx_a:PAD_A, :] = ua_scr[:, PAD_A + ts - ctx_a:PAD_A + ts, :]
        uc_scr[:, PAD_C - ctx_c:PAD_C, :] = uc_scr[:, PAD_C + ts - ctx_c:PAD_C + ts, :]

    x = x_ref[...].reshape(m, x_ref.shape[-1])
    xn = _rms(x, nmix_ref[...]).astype(BF16)

    c0 = 0
    c1 = 2 * d_conv
    c2 = c1 + 3 * d_att
    c3 = c2 + 3 * d_sc

    za = _dot(xn, win_ref[:, c0:c1])
    u_a = za[:, :d_conv] * jax.nn.sigmoid(za[:, d_conv:])
    ua_scr[:, PAD_A:PAD_A + ts, :] = u_a.reshape(bb, ts, d_conv)
    newa_ref[...] = ua_scr[:, PAD_A + ts - ctx_a:PAD_A + ts, :]

    zc = _dot(xn, win_ref[:, c2:c3])
    g_b = zc[:, :d_sc]
    uc_scr[:, PAD_C:PAD_C + ts, :] = (zc[:, d_sc:2 * d_sc] * zc[:, 2 * d_sc:]).reshape(bb, ts, d_sc)
    newc_ref[...] = uc_scr[:, PAD_C + ts - ctx_c:PAD_C + ts, :]

    zb = _dot(xn, win_ref[:, c1:c2])
    zq = zb[:, :d_att]
    zk = zb[:, d_att:2 * d_att]
    zv = zb[:, 2 * d_att:]
    inv_hd = 1.0 / HEAD_DIM_B
    ssq = _dot((zq * zq).astype(BF16), hsum_ref[...]) * inv_hd
    ssk = _dot((zk * zk).astype(BF16), hsum_ref[...]) * inv_hd
    qh = zq * lax.rsqrt(ssq + EPS) * qn_ref[...]
    kh = zk * lax.rsqrt(ssk + EPS) * kn_ref[...]
    q_ref[...] = (qh * (HEAD_DIM_B ** -0.5)).astype(BF16).reshape(bb, ts, d_att)
    k_ref[...] = kh.astype(BF16).reshape(bb, ts, d_att)
    v_ref[...] = zv.astype(BF16).reshape(bb, ts, d_att)
    kf_ref[...] = kh.reshape(bb, ts, d_att)
    vf_ref[...] = zv.reshape(bb, ts, d_att)

    cab = cab_ref[...]
    lng = lng_ref[...]
    lnb = lnb_ref[...]
    ona = ona_ref[...]
    onc = onc_ref[...]
    taps_a = [caw_ref[j:j + 1, :] for j in range(CONV_A_WIDTH)]
    taps_c = [ccw_ref[j:j + 1, :] for j in range(CONV_C_WIDTH)]
    for b in range(bb):
        for r0 in range(0, ts, CONV_ROWS):
            base = PAD_A - ctx_a + r0
            parts = []
            for res in range(SUBLANES):
                js = [j for j in range(CONV_A_WIDTH) if (base + j) % SUBLANES == res]
                if not js:
                    continue
                first = base + js[0]
                win = ua_scr[b, first:first + (js[-1] - js[0]) + CONV_ROWS, :]
                part = None
                for j in js:
                    off = j - js[0]
                    term = win[off:off + CONV_ROWS, :] * taps_a[j]
                    part = term if part is None else part + term
                parts.append(part)
            conv = parts[0]
            for part in parts[1:]:
                conv = conv + part
            conv = conv + cab
            mu = jnp.mean(conv, axis=-1, keepdims=True)
            cen = conv - mu
            var = jnp.mean(cen * cen, axis=-1, keepdims=True)
            y = cen * lax.rsqrt(var + EPS) * lng + lnb
            y = y * jax.nn.sigmoid(y)
            ya_ref[b, r0:r0 + CONV_ROWS, :] = _rms(y, ona).astype(BF16)

            basec = PAD_C - ctx_c + r0
            convc = None
            for j in range(CONV_C_WIDTH):
                term = uc_scr[b, basec + j:basec + j + CONV_ROWS, :] * taps_c[j]
                convc = term if convc is None else convc + term
            row0 = b * ts + r0
            yc = g_b[row0:row0 + CONV_ROWS, :] * convc
            yc_ref[b, r0:r0 + CONV_ROWS, :] = _rms(yc, onc).astype(BF16)


def _mixer_in(x, ctx_a, ctx_c, nmix, win, caw, cab, lng, lnb, qn, kn, ccw, ona, onc, hsum, *,
              bb, ts, name):
    nb, seq, d = x.shape
    d_conv = caw.shape[-1]
    d_sc = ccw.shape[-1]
    d_att = hsum.shape[0]
    assert nb % bb == 0 and seq % ts == 0 and ts % CONV_ROWS == 0 and ts >= CONV_A_WIDTH - 1
    tail = min(BAND_PAST, seq)
    assert tail == ts, "the last sequence tile must hold exactly the cached band rows"
    nt = seq // ts
    grid = (nb // bb, nt)

    def tok(w):
        return pl.BlockSpec((bb, ts, w), lambda b, t: (b, t, 0))

    def per_seq(rows, w):
        return pl.BlockSpec((bb, rows, w), lambda b, t: (b, 0, 0))

    def const(a):
        return pl.BlockSpec(a.shape, lambda b, t: (0,) * a.ndim)

    kern = functools.partial(_mixer_in_kernel, bb=bb, ts=ts, d_conv=d_conv, d_att=d_att, d_sc=d_sc)
    out_shape = (
        jax.ShapeDtypeStruct((nb, seq, d_conv), BF16),
        jax.ShapeDtypeStruct((nb, seq, d_sc), BF16),
        jax.ShapeDtypeStruct((nb, seq, d_att), BF16),
        jax.ShapeDtypeStruct((nb, seq, d_att), BF16),
        jax.ShapeDtypeStruct((nb, seq, d_att), BF16),
        jax.ShapeDtypeStruct((nb, CONV_A_WIDTH - 1, d_conv), F32),
        jax.ShapeDtypeStruct((nb, CONV_C_WIDTH - 1, d_sc), F32),
        jax.ShapeDtypeStruct((nb, tail, d_att), F32),
        jax.ShapeDtypeStruct((nb, tail, d_att), F32),
    )
    out_specs = (
        tok(d_conv), tok(d_sc), tok(d_att), tok(d_att), tok(d_att),
        per_seq(CONV_A_WIDTH - 1, d_conv), per_seq(CONV_C_WIDTH - 1, d_sc),
        per_seq(tail, d_att), per_seq(tail, d_att),
    )
    in_specs = [
        tok(d), per_seq(CONV_A_WIDTH - 1, d_conv), per_seq(CONV_C_WIDTH - 1, d_sc),
        const(nmix), const(win), const(caw), const(cab), const(lng), const(lnb), const(qn),
        const(kn), const(ccw), const(ona), const(onc), const(hsum),
    ]
    return pl.pallas_call(
        kern, grid=grid, in_specs=in_specs, out_specs=out_specs, out_shape=out_shape,
        scratch_shapes=[pltpu.VMEM((bb, PAD_A + ts, d_conv), F32),
                        pltpu.VMEM((bb, PAD_C + ts, d_sc), F32)],
        compiler_params=pltpu.CompilerParams(
            dimension_semantics=("arbitrary", "arbitrary"), vmem_limit_bytes=VMEM_LIMIT),
        name=name,
    )(x, ctx_a, ctx_c, nmix, win, caw, cab, lng, lnb, qn, kn, ccw, ona, onc, hsum)


def _head_masks():
    lane = lax.broadcasted_iota(jnp.int32, (1, LANES), 1)
    lo = (lane < HEAD_DIM_B).astype(BF16)
    return lo, 1.0 - lo


def _attend_pair(qp, kps, vps, biases, masks):
    out = None
    for hh in range(2):
        qh = qp * masks[hh]
        s = [_dot_nt(qh, kp) + bias[hh] for kp, bias in zip(kps, biases)]
        mx = s[0].max(axis=-1, keepdims=True)
        for sj in s[1:]:
            mx = jnp.maximum(mx, sj.max(axis=-1, keepdims=True))
        e = [jnp.exp(sj - mx) for sj in s]
        den = e[0].sum(axis=-1, keepdims=True)
        for ej in e[1:]:
            den = den + ej.sum(axis=-1, keepdims=True)
        acc = None
        for ej, vp in zip(e, vps):
            term = _dot(ej.astype(BF16), vp * masks[hh])
            acc = term if acc is None else acc + term
        acc = acc / den
        out = acc if out is None else out + acc
    return out


def _band_prompt_kernel(*refs, tq, nkb):
    q_ref = refs[0]
    k_refs = refs[1:1 + nkb]
    v_refs = refs[1 + nkb:1 + 2 * nkb]
    bias_ref, onb_ref, yb_ref = refs[1 + 2 * nkb:]
    masks = _head_masks()
    outs = []
    for p in range(N_HEADS_B // 2):
        cols = slice(p * LANES, (p + 1) * LANES)
        qp = q_ref[0, :, cols]
        kps = [kr[0, :, cols] for kr in k_refs]
        vps = [vr[0, :, cols] for vr in v_refs]
        biases = [[bias_ref[0, 2 * p + hh, :, j * tq:(j + 1) * tq] for hh in range(2)]
                  for j in range(nkb)]
        outs.append(_attend_pair(qp, kps, vps, biases, masks))
    o = jnp.concatenate(outs, axis=-1)
    yb_ref[0] = _rms(o, onb_ref[...]).astype(BF16)


def _band_prompt(q, k, v, bias, onb, *, tq, name):
    nb, seq, d_att = q.shape
    assert BAND_PAST % tq == 0 and seq % tq == 0 and tq % CHUNK == 0
    nkb = BAND_PAST // tq + 1
    grid = (nb, seq // tq)

    def kv_spec(j):
        back = nkb - 1 - j
        return pl.BlockSpec((1, tq, d_att), lambda b, t: (b, jnp.maximum(t - back, 0), 0))

    in_specs = ([pl.BlockSpec((1, tq, d_att), lambda b, t: (b, t, 0))]
                + [kv_spec(j) for j in range(nkb)] + [kv_spec(j) for j in range(nkb)]
                + [pl.BlockSpec((1, N_HEADS_B, tq, nkb * tq),
                                lambda b, t: (jnp.minimum(t, nkb - 1), 0, 0, 0)),
                   pl.BlockSpec(onb.shape, lambda b, t: (0, 0))])
    return pl.pallas_call(
        functools.partial(_band_prompt_kernel, tq=tq, nkb=nkb),
        grid=grid, in_specs=in_specs,
        out_specs=pl.BlockSpec((1, tq, d_att), lambda b, t: (b, t, 0)),
        out_shape=jax.ShapeDtypeStruct((nb, seq, d_att), BF16),
        compiler_params=pltpu.CompilerParams(
            dimension_semantics=("arbitrary", "arbitrary"), vmem_limit_bytes=VMEM_LIMIT),
        name=name,
    )(q, *([k] * nkb), *([v] * nkb), bias, onb)


def _band_sample_kernel(q_ref, kc_ref, kn_ref, vc_ref, vn_ref, bias_ref, onb_ref, yb_ref, *, bb):
    masks = _head_masks()
    past = kc_ref.shape[1]
    onb = onb_ref[...]

    def body(i, carry):
        outs = []
        for p in range(N_HEADS_B // 2):
            cols = slice(p * LANES, (p + 1) * LANES)
            qp = q_ref[i, :, cols]
            kps = [kc_ref[i, :, cols].astype(BF16), kn_ref[i, :, cols]]
            vps = [vc_ref[i, :, cols].astype(BF16), vn_ref[i, :, cols]]
            biases = [[bias_ref[2 * p + hh, :, :past] for hh in range(2)],
                      [bias_ref[2 * p + hh, :, past:] for hh in range(2)]]
            outs.append(_attend_pair(qp, kps, vps, biases, masks))
        o = jnp.concatenate(outs, axis=-1)
        yb_ref[i] = _rms(o, onb).astype(BF16)
        return carry

    lax.fori_loop(0, bb, body, 0)


def _band_sample(q, kc, kn, vc, vn, bias, onb, *, bb, name):
    nb, ts, d_att = q.shape
    past = kc.shape[1]
    assert nb % bb == 0

    def seq_spec(rows):
        return pl.BlockSpec((bb, rows, d_att), lambda b: (b, 0, 0))

    in_specs = [seq_spec(ts), seq_spec(past), seq_spec(ts), seq_spec(past), seq_spec(ts),
                pl.BlockSpec(bias.shape, lambda b: (0, 0, 0)),
                pl.BlockSpec(onb.shape, lambda b: (0, 0))]
    return pl.pallas_call(
        functools.partial(_band_sample_kernel, bb=bb),
        grid=(nb // bb,), in_specs=in_specs, out_specs=seq_spec(ts),
        out_shape=jax.ShapeDtypeStruct((nb, ts, d_att), BF16),
        compiler_params=pltpu.CompilerParams(
            dimension_semantics=("arbitrary",), vmem_limit_bytes=VMEM_LIMIT),
        name=name,
    )(q, kc, kn, vc, vn, bias, onb)


def _band_bias(rel_bias, tq, variants):
    nk = BAND_PAST + tq
    qi = BAND_PAST + jnp.arange(tq)[:, None]
    km = jnp.arange(nk)[None, :]
    idx = jnp.clip(qi - km, -REL_CLIP, REL_CLIP) + REL_CLIP
    qc = qi // CHUNK
    kc = km // CHUNK
    in_band = (kc >= qc - BAND_CHUNKS) & (kc <= qc)
    table = rel_bias[:, idx].astype(F32)
    first_valid = BAND_PAST - jnp.arange(variants)[:, None, None] * tq
    valid = in_band[None] & (km[None] >= first_valid)
    return jnp.where(valid[:, None], table[None], NEG)


def _post_kernel(x_ref, ya_ref, yb_ref, yc_ref, mk_ref, mv_ref, wout_ref, nx_ref, wxq_ref,
                 xqn_ref, wxo_ref, nff_ref, wup_ref, wdn_ref, o_ref, *, bb, ts, ff_chunk):
    m = bb * ts
    d = x_ref.shape[-1]
    x = x_ref[...].reshape(m, d)
    y = jnp.concatenate([ya_ref[...].reshape(m, -1), yb_ref[...].reshape(m, -1),
                         yc_ref[...].reshape(m, -1)], axis=-1)
    x1 = x + _dot(y, wout_ref[...])

    qx = _dot(_rms(x1, nx_ref[...]).astype(BF16), wxq_ref[...])
    xqn = xqn_ref[...]
    heads = []
    for h in range(N_HEADS_X):
        cols = slice(h * HEAD_DIM_X, (h + 1) * HEAD_DIM_X)
        qh = (_rms(qx[:, cols], xqn) * (HEAD_DIM_X ** -0.5)).astype(BF16)
        rows = []
        for b in range(bb):
            s = _dot_nt(qh[b * ts:(b + 1) * ts], mk_ref[b, :, cols])
            e = jnp.exp(s - s.max(axis=-1, keepdims=True))
            den = e.sum(axis=-1, keepdims=True)
            rows.append(_dot(e.astype(BF16), mv_ref[b, :, cols]) / den)
        heads.append(rows[0] if bb == 1 else jnp.concatenate(rows, axis=0))
    o = jnp.concatenate(heads, axis=-1).astype(BF16)
    x2 = x1 + _dot(o, wxo_ref[...])

    xn = _rms(x2, nff_ref[...]).astype(BF16)
    d_ff = wup_ref.shape[1]
    acc = x2
    for c0 in range(0, d_ff, ff_chunk):
        hcol = jnp.maximum(_dot(xn, wup_ref[:, c0:c0 + ff_chunk]), 0.0)
        acc = acc + _dot((hcol * hcol).astype(BF16), wdn_ref[c0:c0 + ff_chunk, :])
    o_ref[...] = acc.reshape(bb, ts, d)


def _post(x, ya, yb, yc, mk, mv, wout, nx, wxq, xqn, wxo, nff, wup, wdn, *, bb, ts, name):
    nb, seq, d = x.shape
    assert nb % bb == 0 and seq % ts == 0
    n_mem, d_mem = mk.shape[1:]
    grid = (nb // bb, seq // ts)

    def tok(w):
        return pl.BlockSpec((bb, ts, w), lambda b, t: (b, t, 0))

    def const(a):
        return pl.BlockSpec(a.shape, lambda b, t: (0,) * a.ndim, pipeline_mode=pl.Buffered(1))

    mem = pl.BlockSpec((bb, n_mem, d_mem), lambda b, t: (b, 0, 0))
    in_specs = [tok(d), tok(ya.shape[-1]), tok(yb.shape[-1]), tok(yc.shape[-1]), mem, mem,
                const(wout), const(nx), const(wxq), const(xqn), const(wxo), const(nff),
                const(wup), const(wdn)]
    return pl.pallas_call(
        functools.partial(_post_kernel, bb=bb, ts=ts, ff_chunk=1024),
        grid=grid, in_specs=in_specs, out_specs=tok(d),
        out_shape=jax.ShapeDtypeStruct((nb, seq, d), F32),
        compiler_params=pltpu.CompilerParams(
            dimension_semantics=("arbitrary", "arbitrary"), vmem_limit_bytes=VMEM_LIMIT),
        name=name,
    )(x, ya, yb, yc, mk, mv, wout, nx, wxq, xqn, wxo, nff, wup, wdn)


def _memory_kv_kernel(mem_ref, nmem_ref, wxk_ref, wxv_ref, xkn_ref, mk_ref, mv_ref):
    mn = _rms(mem_ref[0], nmem_ref[0]).astype(BF16)
    zk = _dot(mn, wxk_ref[0])
    xkn = xkn_ref[0]
    heads = [_rms(zk[:, h * HEAD_DIM_X:(h + 1) * HEAD_DIM_X], xkn) for h in range(N_HEADS_X)]
    mk_ref[0, 0] = jnp.concatenate(heads, axis=-1)
    mv_ref[0, 0] = _dot(mn, wxv_ref[0])


def _memory_kv(mem, nmem, wxk, wxv, xkn):
    depth, d, dk = wxk.shape
    nb, n_mem, _ = mem.shape
    out = jax.ShapeDtypeStruct((depth, nb, n_mem, dk), F32)
    out_spec = pl.BlockSpec((1, 1, n_mem, dk), lambda l, b: (l, b, 0, 0))
    return pl.pallas_call(
        _memory_kv_kernel, grid=(depth, nb),
        in_specs=[pl.BlockSpec((1, n_mem, d), lambda l, b: (b, 0, 0)),
                  pl.BlockSpec((1, 1, d), lambda l, b: (l, 0, 0)),
                  pl.BlockSpec((1, d, dk), lambda l, b: (l, 0, 0)),
                  pl.BlockSpec((1, d, dk), lambda l, b: (l, 0, 0)),
                  pl.BlockSpec((1, 1, HEAD_DIM_X), lambda l, b: (l, 0, 0))],
        out_specs=(out_spec, out_spec), out_shape=(out, out),
        compiler_params=pltpu.CompilerParams(
            dimension_semantics=("arbitrary", "arbitrary"), vmem_limit_bytes=VMEM_LIMIT),
        name="memory_kv",
    )(mem, nmem, wxk, wxv, xkn)


def _tile_config(nb, seq):
    ts = min(BAND_PAST, seq)
    bb = max(1, min(nb, BAND_PAST // ts))
    while nb % bb:
        bb -= 1
    return bb, ts


def kernel(x_prompt, x_sample, cache_conv_a, cache_band_k, cache_band_v, cache_conv_c, cache_mem_k, cache_mem_v, mem_prompt, norm_mix, w_in, conv_a_w, conv_a_b, ln_a_g, ln_a_b, q_norm, k_norm, rel_bias, conv_c_w, out_norm, w_out, norm_x, norm_mem, w_xq, w_xk, w_xv, xq_norm, xk_norm, w_xo, norm_ff, w_up, w_down):
    depth = w_in.shape[0]
    bp, seq_p, d = x_prompt.shape
    bs, seq_s, _ = x_sample.shape
    d_conv = conv_a_w.shape[-1]
    d_sc = conv_c_w.shape[-1]
    d_att = N_HEADS_B * HEAD_DIM_B
    past = cache_band_k.shape[2]

    bb_p, ts_p = _tile_config(bp, seq_p)
    bb_s, ts_s = _tile_config(bs, seq_s)
    tq_p = min(256, seq_p)

    row = lambda a: a[:, None, :]
    w_in_b, w_out_b, w_xq_b, w_xk_b, w_xv_b, w_xo_b, w_up_b, w_down_b = (
        a.astype(BF16) for a in (w_in, w_out, w_xq, w_xk, w_xv, w_xo, w_up, w_down))
    nmix, cab, lng, lnb, nx, nmem, xqn, xkn, nff = (
        row(a) for a in (norm_mix, conv_a_b, ln_a_g, ln_a_b, norm_x, norm_mem, xq_norm, xk_norm,
                         norm_ff))
    qn = row(jnp.tile(q_norm, (1, N_HEADS_B)))
    kn = row(jnp.tile(k_norm, (1, N_HEADS_B)))
    ona = row(out_norm[:, :d_conv])
    onb = row(out_norm[:, d_conv:d_conv + d_att])
    onc = row(out_norm[:, d_conv + d_att:])
    head_id = jnp.arange(d_att) // HEAD_DIM_B
    hsum = (head_id[:, None] == head_id[None, :]).astype(BF16)

    mk_p, mv_p = _memory_kv(mem_prompt, nmem, w_xk_b, w_xv_b, xkn)
    mk_pb, mv_pb = mk_p.astype(BF16), mv_p.astype(BF16)
    mk_sb = cache_mem_k.reshape(depth, bs, -1, d).astype(BF16)
    mv_sb = cache_mem_v.reshape(depth, bs, -1, d).astype(BF16)
    kc_all = cache_band_k.reshape(depth, bs, past, d_att)
    vc_all = cache_band_v.reshape(depth, bs, past, d_att)

    zero_a = jnp.zeros((bp, CONV_A_WIDTH - 1, d_conv), F32)
    zero_c = jnp.zeros((bp, CONV_C_WIDTH - 1, d_sc), F32)

    yp, ys = x_prompt, x_sample
    ca_p, kb_p, vb_p, cc_p = [], [], [], []
    ca_s, kb_s, vb_s, cc_s = [], [], [], []
    for l in range(depth):
        mix_w = (nmix[l], w_in_b[l], conv_a_w[l], cab[l], lng[l], lnb[l], qn[l], kn[l],
                 conv_c_w[l], ona[l], onc[l], hsum)
        post_w = (w_out_b[l], nx[l], w_xq_b[l], xqn[l], w_xo_b[l], nff[l], w_up_b[l], w_down_b[l])
        nvar = BAND_PAST // tq_p + 1
        bias_p = _band_bias(rel_bias[l], tq_p, nvar)
        assert past == BAND_PAST and ts_s == CHUNK, "sample step: one chunk behind a full band"
        bias_s = _band_bias(rel_bias[l], ts_s, BAND_PAST // ts_s + 1)[-1]

        ya, yc, q, k, v, sa, sc, kf, vf = _mixer_in(
            yp, zero_a, zero_c, *mix_w, bb=bb_p, ts=ts_p, name=f"mixer_in_p{l}")
        yb = _band_prompt(q, k, v, bias_p, onb[l], tq=tq_p, name=f"band_p{l}")
        yp = _post(yp, ya, yb, yc, mk_pb[l], mv_pb[l], *post_w, bb=bb_p, ts=ts_p,
                   name=f"post_p{l}")
        ca_p.append(sa); cc_p.append(sc)
        kb_p.append(kf.reshape(bp, -1, N_HEADS_B, HEAD_DIM_B))
        vb_p.append(vf.reshape(bp, -1, N_HEADS_B, HEAD_DIM_B))

        ya, yc, q, k, v, sa, sc, kf, vf = _mixer_in(
            ys, cache_conv_a[l], cache_conv_c[l], *mix_w, bb=bb_s, ts=ts_s, name=f"mixer_in_s{l}")
        yb = _band_sample(q, kc_all[l], k, vc_all[l], v, bias_s, onb[l], bb=bb_s,
                          name=f"band_s{l}")
        ys = _post(ys, ya, yb, yc, mk_sb[l], mv_sb[l], *post_w, bb=bb_s, ts=ts_s,
                   name=f"post_s{l}")
        ca_s.append(sa); cc_s.append(sc)
        keep = min(BAND_PAST, past + seq_s)
        kb_s.append(jnp.concatenate([kc_all[l], kf], axis=1)[:, -keep:]
                    .reshape(bs, keep, N_HEADS_B, HEAD_DIM_B))
        vb_s.append(jnp.concatenate([vc_all[l], vf], axis=1)[:, -keep:]
                    .reshape(bs, keep, N_HEADS_B, HEAD_DIM_B))

    return (yp, ys,
            jnp.stack(ca_p), jnp.stack(kb_p), jnp.stack(vb_p), jnp.stack(cc_p),
            mk_p.reshape(depth, bp, -1, N_HEADS_X, HEAD_DIM_X),
            mv_p.reshape(depth, bp, -1, N_HEADS_X, HEAD_DIM_X),
            jnp.stack(ca_s), jnp.stack(kb_s), jnp.stack(vb_s), jnp.stack(cc_s))
```

```python
import functools
import math

import numpy as np
import jax
import jax.numpy as jnp
from jax import lax
from jax.experimental import pallas as pl
from jax.experimental.pallas import tpu as pltpu

EPS = 1e-6
NEG = -1e30
LOG2E = math.log2(math.e)
CHUNK = 64
BAND_CHUNKS = 8
BAND_PAST = BAND_CHUNKS * CHUNK
REL_CLIP = 128
N_HEADS_B = 8
HEAD_DIM_B = 64
N_HEADS_X = 4
HEAD_DIM_X = 256
CONV_A_WIDTH = 31
CONV_C_WIDTH = 3

SUBLANES = 8
LANES = 128
PAD_A = 32
PAD_C = 8
CONV_ROWS = 64
VMEM_LIMIT = 56 * 1024 * 1024

F32 = jnp.float32
BF16 = jnp.bfloat16


def _rms(x, g):
    return x * lax.rsqrt(jnp.mean(x * x, axis=-1, keepdims=True) + EPS) * g


def _ordering_zero(x):
    bits = lax.bitcast_convert_type(x, jnp.uint32)
    return lax.bitcast_convert_type((bits >> 16) >> 16, F32)


def _dot(a, b):
    return jnp.dot(a, b, preferred_element_type=F32)


def _dot_nt(a, b):
    return lax.dot_general(a, b, (((1,), (1,)), ((), ())), preferred_element_type=F32)


def _layer_spec(a, layer):
    zeros = (0,) * (a.ndim - 1)
    return pl.BlockSpec((None,) + a.shape[1:], lambda *_: (layer,) + zeros)


def _params(n_axes):
    return pltpu.CompilerParams(dimension_semantics=("arbitrary",) * n_axes,
                                vmem_limit_bytes=VMEM_LIMIT)


def _mixer_in_kernel(x_ref, ctxa_ref, ctxc_ref, nmix_ref, win_ref, caw_ref, cab_ref, lng_ref,
                     lnb_ref, qn_ref, kn_ref, ccw_ref, ona_ref, onc_ref, hsum_ref,
                     ya_ref, yc_ref, q_ref, k_ref, v_ref, newa_ref, newc_ref, kf_ref, vf_ref,
                     ua_scr, uc_scr, *, bb, ts, d_conv, d_att, d_sc, transpose_tail):
    t = pl.program_id(1)
    m = bb * ts
    ctx_a = CONV_A_WIDTH - 1
    ctx_c = CONV_C_WIDTH - 1

    @pl.when(t == 0)
    def _():
        ua_scr[:, PAD_A - ctx_a:PAD_A, :] = ctxa_ref[...]
        uc_scr[:, PAD_C - ctx_c:PAD_C, :] = ctxc_ref[...]

    @pl.when(t > 0)
    def _():
        ua_scr[:, PAD_A - ctx_a:PAD_A, :] = ua_scr[:, PAD_A + ts - ctx_a:PAD_A + ts, :]
        uc_scr[:, PAD_C - ctx_c:PAD_C, :] = uc_scr[:, PAD_C + ts - ctx_c:PAD_C + ts, :]

    x = x_ref[...].reshape(m, x_ref.shape[-1])
    xn = _rms(x, nmix_ref[...]).astype(BF16)

    c0 = 0
    c1 = 2 * d_conv
    c2 = c1 + 3 * d_att
    c3 = c2 + 3 * d_sc

    za = _dot(xn, win_ref[:, c0:c1])
    u_a = za[:, :d_conv] * jax.nn.sigmoid(za[:, d_conv:])
    ua_scr[:, PAD_A:PAD_A + ts, :] = u_a.reshape(bb, ts, d_conv)
    newa_ref[...] = ua_scr[:, PAD_A + ts - ctx_a:PAD_A + ts, :]

    held = {}

    def store_tail(ref, val):
        if transpose_tail:
            @pl.when(t == pl.num_programs(1) - 1)
            def _():
                ref[0] = val.T
        else:
            ref[...] = val.reshape(bb, ts, d_att)

    def head_rms(z, gain):
        ss = _dot((z * z).astype(BF16), hsum_ref[...]) * (1.0 / HEAD_DIM_B)
        return z * lax.rsqrt(ss + EPS) * gain

    def token(z):
        return z[0:1, :d_conv]

    def stage_c_conv_input():
        z = _dot(xn, win_ref[:, c2 + d_sc:c3])
        uc_scr[:, PAD_C:PAD_C + ts, :] = (z[:, :d_sc] * z[:, d_sc:]).reshape(bb, ts, d_sc)
        newc_ref[...] = uc_scr[:, PAD_C + ts - ctx_c:PAD_C + ts, :]
        return token(z)

    def stage_c_gate():
        held["g_b"] = _dot(xn, win_ref[:, c2:c2 + d_sc])
        return token(held["g_b"])

    def stage_q_proj():
        held["zq"] = _dot(xn, win_ref[:, c1:c1 + d_att])
        return token(held["zq"])

    def stage_q_norm():
        qh = head_rms(held.pop("zq"), qn_ref[...]) * (HEAD_DIM_B ** -0.5 * LOG2E)
        q_ref[...] = qh.astype(BF16).reshape(bb, ts, d_att)
        return token(qh)

    def stage_k_proj():
        held["zk"] = _dot(xn, win_ref[:, c1 + d_att:c1 + 2 * d_att])
        return token(held["zk"])

    def stage_k_norm():
        kh = head_rms(held.pop("zk"), kn_ref[...])
        k_ref[...] = kh.astype(BF16).reshape(bb, ts, d_att)
        store_tail(kf_ref, kh)
        return token(kh)

    def stage_v_proj():
        zv = _dot(xn, win_ref[:, c1 + 2 * d_att:c2])
        v_ref[...] = zv.astype(BF16).reshape(bb, ts, d_att)
        store_tail(vf_ref, zv)
        return token(zv)

    stages = [stage_c_conv_input, stage_c_gate, stage_q_proj, stage_q_norm, stage_k_proj,
              stage_k_norm, stage_v_proj]
    c_conv_lag = 2

    cab = cab_ref[...]
    lng = lng_ref[...]
    lnb = lnb_ref[...]
    ona = ona_ref[...]
    onc = onc_ref[...]
    taps_a = [caw_ref[j:j + 1, :] for j in range(CONV_A_WIDTH)]
    taps_c = [ccw_ref[j:j + 1, :] for j in range(CONV_C_WIDTH)]

    def conformer_conv_block(b, r0, after):
        taps = taps_a if after is None else [tap + _ordering_zero(after) for tap in taps_a]
        base = PAD_A - ctx_a + r0
        parts = []
        for res in range(SUBLANES):
            js = [j for j in range(CONV_A_WIDTH) if (base + j) % SUBLANES == res]
            if not js:
                continue
            first = base + js[0]
            win = ua_scr[b, first:first + (js[-1] - js[0]) + CONV_ROWS, :]
            part = None
            for j in js:
                off = j - js[0]
                term = win[off:off + CONV_ROWS, :] * taps[j]
                part = term if part is None else part + term
            parts.append(part)
        conv = parts[0]
        for part in parts[1:]:
            conv = conv + part
        conv = conv + cab
        mu = jnp.mean(conv, axis=-1, keepdims=True)
        cen = conv - mu
        var = jnp.mean(cen * cen, axis=-1, keepdims=True)
        y = cen * lax.rsqrt(var + EPS) * lng + lnb
        y = y * jax.nn.sigmoid(y)
        ya_ref[b, r0:r0 + CONV_ROWS, :] = _rms(y, ona).astype(BF16)

    def short_conv_block(b, r0):
        basec = PAD_C - ctx_c + r0
        convc = None
        for j in range(CONV_C_WIDTH):
            term = uc_scr[b, basec + j:basec + j + CONV_ROWS, :] * taps_c[j]
            convc = term if convc is None else convc + term
        row0 = b * ts + r0
        yc = held["g_b"][row0:row0 + CONV_ROWS, :] * convc
        yc_ref[b, r0:r0 + CONV_ROWS, :] = _rms(yc, onc).astype(BF16)

    blocks = [(b, r0) for b in range(bb) for r0 in range(0, ts, CONV_ROWS)]
    pending_c = list(blocks)
    after = None
    for n, blk in enumerate(blocks):
        conformer_conv_block(*blk, after)
        if n < len(stages):
            after = stages[n]()
        if n >= c_conv_lag - 1:
            short_conv_block(*pending_c.pop(0))
    for stage in stages[len(blocks):]:
        stage()
    for blk in pending_c:
        short_conv_block(*blk)


def _mixer_in(x, ctx_a, ctx_c, ctx_layer, layer_w, hsum, layer, *, bb, ts, transpose_tail, name):
    nb, seq, d = x.shape
    d_conv = layer_w[2].shape[-1]
    d_sc = layer_w[8].shape[-1]
    d_att = hsum.shape[0]
    assert nb % bb == 0 and seq % ts == 0 and ts % CONV_ROWS == 0 and ts >= CONV_A_WIDTH - 1
    tail = min(BAND_PAST, seq)
    assert tail == ts, "the last sequence tile must hold exactly the cached band rows"
    assert not transpose_tail or bb == 1
    grid = (nb // bb, seq // ts)

    def tok(w):
        return pl.BlockSpec((bb, ts, w), lambda b, t: (b, t, 0))

    def per_seq(rows, w):
        return pl.BlockSpec((bb, rows, w), lambda b, t: (b, 0, 0))

    def ctx_spec(a):
        if ctx_layer is None:
            return per_seq(a.shape[-2], a.shape[-1])
        return pl.BlockSpec((None, bb) + a.shape[2:], lambda b, t: (ctx_layer, b, 0, 0))

    kern = functools.partial(_mixer_in_kernel, bb=bb, ts=ts, d_conv=d_conv, d_att=d_att, d_sc=d_sc,
                             transpose_tail=transpose_tail)
    tail_shape = (nb, d_att, tail) if transpose_tail else (nb, tail, d_att)
    tail_spec = per_seq(d_att, tail) if transpose_tail else per_seq(tail, d_att)
    out_shape = (
        jax.ShapeDtypeStruct((nb, seq, d_conv), BF16),
        jax.ShapeDtypeStruct((nb, seq, d_sc), BF16),
        jax.ShapeDtypeStruct((nb, seq, d_att), BF16),
        jax.ShapeDtypeStruct((nb, seq, d_att), BF16),
        jax.ShapeDtypeStruct((nb, seq, d_att), BF16),
        jax.ShapeDtypeStruct((nb, CONV_A_WIDTH - 1, d_conv), F32),
        jax.ShapeDtypeStruct((nb, CONV_C_WIDTH - 1, d_sc), F32),
        jax.ShapeDtypeStruct(tail_shape, F32),
        jax.ShapeDtypeStruct(tail_shape, F32),
    )
    out_specs = (
        tok(d_conv), tok(d_sc), tok(d_att), tok(d_att), tok(d_att),
        per_seq(CONV_A_WIDTH - 1, d_conv), per_seq(CONV_C_WIDTH - 1, d_sc), tail_spec, tail_spec,
    )
    in_specs = ([tok(d), ctx_spec(ctx_a), ctx_spec(ctx_c)]
                + [_layer_spec(a, layer) for a in layer_w]
                + [pl.BlockSpec(hsum.shape, lambda b, t: (0, 0))])
    return pl.pallas_call(
        kern, grid=grid, in_specs=in_specs, out_specs=out_specs, out_shape=out_shape,
        scratch_shapes=[pltpu.VMEM((bb, PAD_A + ts, d_conv), F32),
                        pltpu.VMEM((bb, PAD_C + ts, d_sc), F32)],
        compiler_params=_params(2), name=name,
    )(x, ctx_a, ctx_c, *layer_w, hsum)


def _pair_consts():
    lane = lax.broadcasted_iota(jnp.int32, (1, LANES), 1)
    lo = lane < HEAD_DIM_B
    hi = lane >= HEAD_DIM_B
    keep_bf = (lo.astype(BF16), hi.astype(BF16))
    keep_f = (lo.astype(F32), hi.astype(F32))
    den_lane = (HEAD_DIM_B, 0)
    ones_bf = tuple((lane == dl).astype(BF16) for dl in den_lane)
    return keep_bf, keep_f, ones_bf, den_lane


def _probabilities(scores):
    mx = scores[0].max(axis=-1, keepdims=True)
    for sj in scores[1:]:
        mx = jnp.maximum(mx, sj.max(axis=-1, keepdims=True))
    return [jnp.exp2(sj - mx).astype(BF16) for sj in scores]


def _skewed_heads(n_heads, score, weighted_sum):
    scores, probs, outs = {}, {}, []
    for i in range(n_heads + 2):
        if i < n_heads:
            scores[i] = score(i)
        if 0 <= i - 2 < n_heads:
            outs.append(weighted_sum(i - 2, probs.pop(i - 2)))
        if 0 <= i - 1 < n_heads:
            probs[i - 1] = _probabilities(scores.pop(i - 1))
    return outs


def _pair_outputs(terms):
    return jnp.concatenate([terms[2 * p] + terms[2 * p + 1] for p in range(len(terms) // 2)],
                           axis=-1)


def _band_prompt_kernel(*refs, tq, nkb):
    q_ref = refs[0]
    k_refs = refs[1:1 + nkb]
    v_refs = refs[1 + nkb:1 + 2 * nkb]
    bias_ref, onb_ref, yb_ref = refs[1 + 2 * nkb:]
    keep_bf, keep_f, ones_bf, den_lane = _pair_consts()

    def score(h):
        p, hh = divmod(h, 2)
        cols = slice(p * LANES, (p + 1) * LANES)
        qh = q_ref[0, :, cols] * keep_bf[hh]
        return [_dot_nt(qh, kr[0, :, cols]) + bias_ref[0, h, :, j * tq:(j + 1) * tq]
                for j, kr in enumerate(k_refs)]

    def weighted_sum(h, probs):
        p, hh = divmod(h, 2)
        cols = slice(p * LANES, (p + 1) * LANES)
        acc = None
        for e, vr in zip(probs, v_refs):
            term = _dot(e, vr[0, :, cols] * keep_bf[hh] + ones_bf[hh])
            acc = term if acc is None else acc + term
        den = acc[:, den_lane[hh]:den_lane[hh] + 1]
        return acc * keep_f[hh] / den

    o = _pair_outputs(_skewed_heads(N_HEADS_B, score, weighted_sum))
    yb_ref[0] = _rms(o, onb_ref[...]).astype(BF16)


def _band_prompt(q, k, v, bias, onb, layer, *, tq, name):
    nb, seq, d_att = q.shape
    assert BAND_PAST % tq == 0 and seq % tq == 0 and tq % CHUNK == 0
    nkb = BAND_PAST // tq + 1
    grid = (nb, seq // tq)

    def kv_spec(j):
        back = nkb - 1 - j
        return pl.BlockSpec((1, tq, d_att), lambda b, t: (b, jnp.maximum(t - back, 0), 0))

    in_specs = ([pl.BlockSpec((1, tq, d_att), lambda b, t: (b, t, 0))]
                + [kv_spec(j) for j in range(nkb)] + [kv_spec(j) for j in range(nkb)]
                + [pl.BlockSpec((None, 1, N_HEADS_B, tq, nkb * tq),
                                lambda b, t: (layer, jnp.minimum(t, nkb - 1), 0, 0, 0)),
                   _layer_spec(onb, layer)])
    return pl.pallas_call(
        functools.partial(_band_prompt_kernel, tq=tq, nkb=nkb),
        grid=grid, in_specs=in_specs,
        out_specs=pl.BlockSpec((1, tq, d_att), lambda b, t: (b, t, 0)),
        out_shape=jax.ShapeDtypeStruct((nb, seq, d_att), BF16),
        compiler_params=_params(2), name=name,
    )(q, *([k] * nkb), *([v] * nkb), bias, onb)


def _band_sample_kernel(q_ref, kn_ref, vn_ref, kf_ref, vf_ref, kct_ref, vct_ref, bias_ref, onb_ref,
                        yb_ref, knew_ref, vnew_ref, *, bb):
    ts = q_ref.shape[1]
    past = kct_ref.shape[-1]
    keep_bf, keep_f, ones_bf, den_lane = _pair_consts()
    row = lax.broadcasted_iota(jnp.int32, (LANES, past), 0)
    keep_rows = ((row < HEAD_DIM_B).astype(BF16), (row >= HEAD_DIM_B).astype(BF16))
    ones_rows = tuple((row == dl).astype(BF16) for dl in den_lane)
    lane = lax.broadcasted_iota(jnp.int32, (1, LANES), 1)
    onb = onb_ref[...]

    def body(i, carry):
        kct = kct_ref[i]
        vct = vct_ref[i]

        def score(h):
            p, hh = divmod(h, 2)
            cols = slice(p * LANES, (p + 1) * LANES)
            qh = q_ref[i, :, cols] * keep_bf[hh]
            ktp = kct[2 * p:2 * p + 2].reshape(LANES, past).astype(BF16)
            return [_dot(qh, ktp) + bias_ref[h, :, :past],
                    _dot_nt(qh, kn_ref[i, :, cols]) + bias_ref[h, :, past:]]

        def weighted_sum(h, probs):
            p, hh = divmod(h, 2)
            cols = slice(p * LANES, (p + 1) * LANES)
            vtp = vct[2 * p:2 * p + 2].reshape(LANES, past).astype(BF16)
            acc = (_dot_nt(probs[0], vtp * keep_rows[hh] + ones_rows[hh])
                   + _dot(probs[1], vn_ref[i, :, cols] * keep_bf[hh] + ones_bf[hh]))
            den = acc[:, den_lane[hh]:den_lane[hh] + 1]
            return acc * keep_f[hh] / den

        o = _pair_outputs(_skewed_heads(N_HEADS_B, score, weighted_sum))
        yb_ref[i] = _rms(o, onb).astype(BF16)

        for old, new_ref, out_ref in ((kct, kf_ref, knew_ref), (vct, vf_ref, vnew_ref)):
            old2 = old.reshape(N_HEADS_B * HEAD_DIM_B, past)
            shifted = pltpu.roll(old2, past - ts, axis=1)
            pad_rows = jnp.zeros((LANES - ts, new_ref.shape[-1]), F32)
            new_t = jnp.concatenate([pad_rows, new_ref[i]], axis=0).T
            last = jnp.where(lane >= LANES - ts, new_t, shifted[:, past - LANES:])
            out_ref[i, :, :, :past - LANES] = shifted[:, :past - LANES].reshape(
                N_HEADS_B, HEAD_DIM_B, past - LANES)
            out_ref[i, :, :, past - LANES:] = last.reshape(N_HEADS_B, HEAD_DIM_B, LANES)
        return carry

    lax.fori_loop(0, bb, body, 0)


def _band_sample(q, kn, vn, kf, vf, kct, vct, bias, onb, layer, *, bb, name):
    nb, ts, d_att = q.shape
    past = kct.shape[-1]
    assert nb % bb == 0 and ts < LANES and past >= 2 * LANES

    def seq_spec(rows):
        return pl.BlockSpec((bb, rows, d_att), lambda b: (b, 0, 0))

    cache_in = pl.BlockSpec((None, bb, N_HEADS_B, HEAD_DIM_B, past), lambda b: (layer, b, 0, 0, 0))
    cache_out = pl.BlockSpec((bb, N_HEADS_B, HEAD_DIM_B, past), lambda b: (b, 0, 0, 0))
    cache_shape = jax.ShapeDtypeStruct((nb, N_HEADS_B, HEAD_DIM_B, past), F32)
    in_specs = [seq_spec(ts)] * 5 + [cache_in, cache_in, _layer_spec(bias, layer),
                                     _layer_spec(onb, layer)]
    return pl.pallas_call(
        functools.partial(_band_sample_kernel, bb=bb),
        grid=(nb // bb,), in_specs=in_specs, out_specs=(seq_spec(ts), cache_out, cache_out),
        out_shape=(jax.ShapeDtypeStruct((nb, ts, d_att), BF16), cache_shape, cache_shape),
        compiler_params=_params(1), name=name,
    )(q, kn, vn, kf, vf, kct, vct, bias, onb)


def _band_bias(rel_bias, tq, variants, first_variant=0):
    depth, n_heads, n_rel = rel_bias.shape
    nq, nk = tq // CHUNK, (BAND_PAST + tq) // CHUNK
    a = np.arange(CHUNK)
    near = (REL_CLIP + CHUNK - 1) // CHUNK + 1
    idx = np.stack([np.clip(dc * CHUNK + a[:, None] - a[None, :], -REL_CLIP, REL_CLIP) + REL_CLIP
                    for dc in range(near)])
    onehot = jnp.asarray(np.eye(n_rel, dtype=np.float32)[idx])
    tiles = jnp.einsum("nabr,lhr->lhnab", onehot, rel_bias.astype(F32),
                       precision=lax.Precision.HIGHEST)
    far = rel_bias[:, :, n_rel - 1].astype(F32)
    qc = BAND_CHUNKS + np.arange(nq)[:, None]
    kc = np.arange(nk)[None, :]
    dist = qc - kc
    first_valid = BAND_PAST - (first_variant + np.arange(variants))[:, None, None] * tq
    valid = ((dist >= 0) & (dist <= BAND_CHUNKS))[None] & (kc[None] * CHUNK >= first_valid)
    kind = np.where(valid, np.minimum(dist, near)[None], near + 1)
    kind = jnp.asarray(kind)[None, :, None, :, None, :, None]
    table = jnp.where(kind == near, far[:, None, :, None, None, None, None], NEG)
    for dc in range(near):
        tile = tiles[:, None, :, dc, None, :, None, :]
        table = jnp.where(kind == dc, tile, table)
    table = table.reshape(depth, variants, n_heads, tq, BAND_PAST + tq)
    return jnp.where(table > 0.5 * NEG, table * LOG2E, NEG)


def _post_kernel(x_ref, ya_ref, yb_ref, yc_ref, mk_ref, mv_ref, wout_ref, nx_ref, wxq_ref,
                 xqn_ref, wxo_ref, nff_ref, wup_ref, wdn_ref, o_ref, *, bb, ts, ff_chunk):
    m = bb * ts
    d = x_ref.shape[-1]
    x = x_ref[...].reshape(m, d)
    y = jnp.concatenate([ya_ref[...].reshape(m, -1), yb_ref[...].reshape(m, -1),
                         yc_ref[...].reshape(m, -1)], axis=-1)
    x1 = x + _dot(y, wout_ref[...])

    qx = _dot(_rms(x1, nx_ref[...]).astype(BF16), wxq_ref[...])
    xqn = xqn_ref[...]
    heads = []
    for h in range(N_HEADS_X):
        cols = slice(h * HEAD_DIM_X, (h + 1) * HEAD_DIM_X)
        qh = (_rms(qx[:, cols], xqn) * (HEAD_DIM_X ** -0.5)).astype(BF16)
        rows = []
        for b in range(bb):
            s = _dot_nt(qh[b * ts:(b + 1) * ts], mk_ref[b, h])
            e = jnp.exp(s - s.max(axis=-1, keepdims=True))
            den = e.sum(axis=-1, keepdims=True)
            rows.append(_dot(e.astype(BF16), mv_ref[b, h]) / den)
        heads.append(rows[0] if bb == 1 else jnp.concatenate(rows, axis=0))
    o = jnp.concatenate(heads, axis=-1).astype(BF16)
    x2 = x1 + _dot(o, wxo_ref[...])

    xn = _rms(x2, nff_ref[...]).astype(BF16)
    d_ff = wup_ref.shape[1]
    acc = x2
    for c0 in range(0, d_ff, ff_chunk):
        hcol = jnp.maximum(_dot(xn, wup_ref[:, c0:c0 + ff_chunk]), 0.0)
        acc = acc + _dot((hcol * hcol).astype(BF16), wdn_ref[c0:c0 + ff_chunk, :])
    o_ref[...] = acc.reshape(bb, ts, d)


def _post(x, ya, yb, yc, mk, mv, layer_w, layer, *, bb, ts, name):
    nb, seq, d = x.shape
    assert nb % bb == 0 and seq % ts == 0
    grid = (nb // bb, seq // ts)

    def tok(w):
        return pl.BlockSpec((bb, ts, w), lambda b, t: (b, t, 0))

    def weight(a):
        zeros = (0,) * (a.ndim - 1)
        return pl.BlockSpec((None,) + a.shape[1:], lambda b, t: (layer,) + zeros,
                            pipeline_mode=pl.Buffered(1))

    mem = pl.BlockSpec((None, bb) + mk.shape[2:], lambda b, t: (layer, b, 0, 0, 0))
    in_specs = ([tok(d), tok(ya.shape[-1]), tok(yb.shape[-1]), tok(yc.shape[-1]), mem, mem]
                + [weight(a) for a in layer_w])
    return pl.pallas_call(
        functools.partial(_post_kernel, bb=bb, ts=ts, ff_chunk=1024),
        grid=grid, in_specs=in_specs, out_specs=tok(d),
        out_shape=jax.ShapeDtypeStruct((nb, seq, d), F32),
        compiler_params=_params(2), name=name,
    )(x, ya, yb, yc, mk, mv, *layer_w)


def _memory_kv_kernel(mem_ref, nmem_ref, wxk_ref, wxv_ref, xkn_ref, mk_ref, mv_ref, mkh_ref,
                      mvh_ref):
    mn = _rms(mem_ref[0], nmem_ref[0]).astype(BF16)
    zk = _dot(mn, wxk_ref[0])
    zv = _dot(mn, wxv_ref[0])
    xkn = xkn_ref[0]
    heads = [_rms(zk[:, h * HEAD_DIM_X:(h + 1) * HEAD_DIM_X], xkn) for h in range(N_HEADS_X)]
    mk_ref[0, 0] = jnp.concatenate(heads, axis=-1)
    mv_ref[0, 0] = zv
    for h in range(N_HEADS_X):
        mkh_ref[0, 0, h] = heads[h].astype(BF16)
        mvh_ref[0, 0, h] = zv[:, h * HEAD_DIM_X:(h + 1) * HEAD_DIM_X].astype(BF16)


def _memory_kv(mem, nmem, wxk, wxv, xkn):
    depth, d, dk = wxk.shape
    nb, n_mem, _ = mem.shape
    out = jax.ShapeDtypeStruct((depth, nb, n_mem, dk), F32)
    out_h = jax.ShapeDtypeStruct((depth, nb, N_HEADS_X, n_mem, HEAD_DIM_X), BF16)
    out_spec = pl.BlockSpec((1, 1, n_mem, dk), lambda l, b: (l, b, 0, 0))
    out_h_spec = pl.BlockSpec((1, 1, N_HEADS_X, n_mem, HEAD_DIM_X), lambda l, b: (l, b, 0, 0, 0))
    return pl.pallas_call(
        _memory_kv_kernel, grid=(depth, nb),
        in_specs=[pl.BlockSpec((1, n_mem, d), lambda l, b: (b, 0, 0)),
                  pl.BlockSpec((1, 1, d), lambda l, b: (l, 0, 0)),
                  pl.BlockSpec((1, d, dk), lambda l, b: (l, 0, 0)),
                  pl.BlockSpec((1, d, dk), lambda l, b: (l, 0, 0)),
                  pl.BlockSpec((1, 1, HEAD_DIM_X), lambda l, b: (l, 0, 0))],
        out_specs=(out_spec, out_spec, out_h_spec, out_h_spec),
        out_shape=(out, out, out_h, out_h),
        compiler_params=_params(2), name="memory_kv",
    )(mem, nmem, wxk, wxv, xkn)


def _tile_config(nb, seq):
    ts = min(BAND_PAST, seq)
    bb = max(1, min(nb, BAND_PAST // ts))
    while nb % bb:
        bb -= 1
    return bb, ts


def kernel(x_prompt, x_sample, cache_conv_a, cache_band_k, cache_band_v, cache_conv_c, cache_mem_k, cache_mem_v, mem_prompt, norm_mix, w_in, conv_a_w, conv_a_b, ln_a_g, ln_a_b, q_norm, k_norm, rel_bias, conv_c_w, out_norm, w_out, norm_x, norm_mem, w_xq, w_xk, w_xv, xq_norm, xk_norm, w_xo, norm_ff, w_up, w_down):
    depth = w_in.shape[0]
    bp, seq_p, d = x_prompt.shape
    bs, seq_s, _ = x_sample.shape
    d_conv = conv_a_w.shape[-1]
    d_sc = conv_c_w.shape[-1]
    d_att = N_HEADS_B * HEAD_DIM_B
    past = cache_band_k.shape[2]
    assert past == BAND_PAST and seq_s == CHUNK, "sample step: one chunk behind a full band"

    bb_p, ts_p = _tile_config(bp, seq_p)
    bb_s, ts_s = _tile_config(bs, seq_s)
    tq_p = min(256, seq_p)
    bb_band_s = 4 if bs % 4 == 0 else 1

    row = lambda a: a[:, None, :]
    w_in_b, w_out_b, w_xq_b, w_xk_b, w_xv_b, w_xo_b, w_up_b, w_down_b = (
        a.astype(BF16) for a in (w_in, w_out, w_xq, w_xk, w_xv, w_xo, w_up, w_down))
    nmem, xkn = row(norm_mem), row(xk_norm)
    onb = row(out_norm[:, d_conv:d_conv + d_att])
    mix_w = (row(norm_mix), w_in_b, conv_a_w, row(conv_a_b), row(ln_a_g), row(ln_a_b),
             row(jnp.tile(q_norm, (1, N_HEADS_B))), row(jnp.tile(k_norm, (1, N_HEADS_B))),
             conv_c_w, row(out_norm[:, :d_conv]), row(out_norm[:, d_conv + d_att:]))
    post_w = (w_out_b, row(norm_x), w_xq_b, row(xq_norm), w_xo_b, row(norm_ff), w_up_b, w_down_b)
    head_id = jnp.arange(d_att) // HEAD_DIM_B
    hsum = (head_id[:, None] == head_id[None, :]).astype(BF16)

    bias_p = _band_bias(rel_bias, tq_p, BAND_PAST // tq_p + 1)
    bias_s = _band_bias(rel_bias, ts_s, 1, first_variant=BAND_PAST // ts_s)[:, 0]

    mk_p, mv_p, mk_ph, mv_ph = _memory_kv(mem_prompt, nmem, w_xk_b, w_xv_b, xkn)
    mk_sh = jnp.transpose(cache_mem_k, (0, 1, 3, 2, 4)).astype(BF16)
    mv_sh = jnp.transpose(cache_mem_v, (0, 1, 3, 2, 4)).astype(BF16)
    kct = jnp.transpose(cache_band_k, (0, 1, 3, 4, 2))
    vct = jnp.transpose(cache_band_v, (0, 1, 3, 4, 2))

    zero_a = jnp.zeros((bp, CONV_A_WIDTH - 1, d_conv), F32)
    zero_c = jnp.zeros((bp, CONV_C_WIDTH - 1, d_sc), F32)

    yp, ys = x_prompt, x_sample
    ca_p, kb_p, vb_p, cc_p = [], [], [], []
    ca_s, kb_s, vb_s, cc_s = [], [], [], []
    for l in range(depth):
        ya, yc, q, k, v, sa, sc, kft, vft = _mixer_in(
            yp, zero_a, zero_c, None, mix_w, hsum, l, bb=bb_p, ts=ts_p, transpose_tail=True,
            name=f"mixer_in_p{l}")
        yb = _band_prompt(q, k, v, bias_p, onb, l, tq=tq_p, name=f"band_p{l}")
        yp = _post(yp, ya, yb, yc, mk_ph, mv_ph, post_w, l, bb=bb_p, ts=ts_p, name=f"post_p{l}")
        ca_p.append(sa); cc_p.append(sc); kb_p.append(kft); vb_p.append(vft)

        ya, yc, q, k, v, sa, sc, kf, vf = _mixer_in(
            ys, cache_conv_a, cache_conv_c, l, mix_w, hsum, l, bb=bb_s, ts=ts_s,
            transpose_tail=False, name=f"mixer_in_s{l}")
        yb, knew, vnew = _band_sample(q, k, v, kf, vf, kct, vct, bias_s, onb, l, bb=bb_band_s,
                                      name=f"band_s{l}")
        ys = _post(ys, ya, yb, yc, mk_sh, mv_sh, post_w, l, bb=bb_s, ts=ts_s, name=f"post_s{l}")
        ca_s.append(sa); cc_s.append(sc); kb_s.append(knew); vb_s.append(vnew)

    def band_rows_major(parts, nb):
        stacked = jnp.stack(parts).reshape(depth, nb, N_HEADS_B, HEAD_DIM_B, -1)
        return jnp.transpose(stacked, (0, 1, 4, 2, 3))

    return (yp, ys,
            jnp.stack(ca_p), band_rows_major(kb_p, bp), band_rows_major(vb_p, bp), jnp.stack(cc_p),
            mk_p.reshape(depth, bp, -1, N_HEADS_X, HEAD_DIM_X),
            mv_p.reshape(depth, bp, -1, N_HEADS_X, HEAD_DIM_X),
            jnp.stack(ca_s), band_rows_major(kb_s, bs), band_rows_major(vb_s, bs), jnp.stack(cc_s))
```

```python
import functools
import math

import jax
import jax.numpy as jnp
from jax import lax
from jax.experimental import pallas as pl
from jax.experimental.pallas import tpu as pltpu

EPS = 1e-6
NEG = -1e30
LOG2E = math.log2(math.e)
CHUNK = 64
BAND_CHUNKS = 8
BAND_PAST = BAND_CHUNKS * CHUNK
REL_CLIP = 128
N_HEADS_B = 8
HEAD_DIM_B = 64
N_HEADS_X = 4
HEAD_DIM_X = 256
CONV_A_WIDTH = 31
CONV_C_WIDTH = 3

SUBLANES = 8
LANES = 128
PAD_A = 32
PAD_C = 8
CONV_ROWS = 64
MAX_TQ = 256
BIAS_ROW = 1024
VMEM_LIMIT = 56 * 1024 * 1024

F32 = jnp.float32
BF16 = jnp.bfloat16


def _rms(x, g):
    return x * lax.rsqrt(jnp.mean(x * x, axis=-1, keepdims=True) + EPS) * g


def _ordering_zero(x):
    bits = lax.bitcast_convert_type(x, jnp.uint32)
    return lax.bitcast_convert_type((bits >> 16) >> 16, F32)


def _dot(a, b):
    return jnp.dot(a, b, preferred_element_type=F32)


def _dot_nt(a, b):
    return lax.dot_general(a, b, (((1,), (1,)), ((), ())), preferred_element_type=F32)


def _layer_spec(a, layer):
    zeros = (0,) * (a.ndim - 1)
    return pl.BlockSpec((None,) + a.shape[1:], lambda *_: (layer,) + zeros)


def _params(n_axes):
    return pltpu.CompilerParams(dimension_semantics=("arbitrary",) * n_axes,
                                vmem_limit_bytes=VMEM_LIMIT)


def _mixer_in_kernel(x_ref, ctxa_ref, ctxc_ref, nmix_ref, win_ref, caw_ref, cab_ref, lng_ref,
                     lnb_ref, qn_ref, kn_ref, ccw_ref, ona_ref, onc_ref, hsum_ref,
                     ya_ref, yc_ref, q_ref, k_ref, v_ref, newa_ref, newc_ref, kf_ref, vf_ref,
                     ua_scr, uc_scr, *, bb, ts, d_conv, d_att, d_sc):
    t = pl.program_id(1)
    m = bb * ts
    ctx_a = CONV_A_WIDTH - 1
    ctx_c = CONV_C_WIDTH - 1

    @pl.when(t == 0)
    def _():
        ua_scr[:, PAD_A - ctx_a:PAD_A, :] = ctxa_ref[...]
        uc_scr[:, PAD_C - ctx_c:PAD_C, :] = ctxc_ref[...]

    @pl.when(t > 0)
    def _():
        ua_scr[:, PAD_A - ctx_a:PAD_A, :] = ua_scr[:, PAD_A + ts - ctx_a:PAD_A + ts, :]
        uc_scr[:, PAD_C - ctx_c:PAD_C, :] = uc_scr[:, PAD_C + ts - ctx_c:PAD_C + ts, :]

    x = x_ref[...].reshape(m, x_ref.shape[-1])
    xn = _rms(x, nmix_ref[...]).astype(BF16)

    c0 = 0
    c1 = 2 * d_conv
    c2 = c1 + 3 * d_att
    c3 = c2 + 3 * d_sc

    za = _dot(xn, win_ref[:, c0:c1])
    u_a = za[:, :d_conv] * jax.nn.sigmoid(za[:, d_conv:])
    ua_scr[:, PAD_A:PAD_A + ts, :] = u_a.reshape(bb, ts, d_conv)
    newa_ref[...] = ua_scr[:, PAD_A + ts - ctx_a:PAD_A + ts, :]

    held = {}

    def store_tail(ref, val):
        ref[...] = val.reshape(bb, ts, d_att)

    def head_rms(z, gain):
        ss = _dot((z * z).astype(BF16), hsum_ref[...]) * (1.0 / HEAD_DIM_B)
        return z * lax.rsqrt(ss + EPS) * gain

    def token(z):
        return z[0:1, :d_conv]

    def stage_c_conv_input():
        z = _dot(xn, win_ref[:, c2 + d_sc:c3])
        uc_scr[:, PAD_C:PAD_C + ts, :] = (z[:, :d_sc] * z[:, d_sc:]).reshape(bb, ts, d_sc)
        newc_ref[...] = uc_scr[:, PAD_C + ts - ctx_c:PAD_C + ts, :]
        return token(z)

    def stage_c_gate():
        held["g_b"] = _dot(xn, win_ref[:, c2:c2 + d_sc])
        return token(held["g_b"])

    def stage_q_proj():
        held["zq"] = _dot(xn, win_ref[:, c1:c1 + d_att])
        return token(held["zq"])

    def stage_q_norm():
        qh = head_rms(held.pop("zq"), qn_ref[...]) * (HEAD_DIM_B ** -0.5 * LOG2E)
        q_ref[...] = qh.astype(BF16).reshape(bb, ts, d_att)
        return token(qh)

    def stage_k_proj():
        held["zk"] = _dot(xn, win_ref[:, c1 + d_att:c1 + 2 * d_att])
        return token(held["zk"])

    def stage_k_norm():
        kh = head_rms(held.pop("zk"), kn_ref[...])
        k_ref[...] = kh.astype(BF16).reshape(bb, ts, d_att)
        store_tail(kf_ref, kh)
        return token(kh)

    def stage_v_proj():
        zv = _dot(xn, win_ref[:, c1 + 2 * d_att:c2])
        v_ref[...] = zv.astype(BF16).reshape(bb, ts, d_att)
        store_tail(vf_ref, zv)
        return token(zv)

    stages = [stage_c_conv_input, stage_c_gate, stage_q_proj, stage_q_norm, stage_k_proj,
              stage_k_norm, stage_v_proj]
    c_conv_lag = 2

    cab = cab_ref[...]
    lng = lng_ref[...]
    lnb = lnb_ref[...]
    ona = ona_ref[...]
    onc = onc_ref[...]
    taps_a = [caw_ref[j:j + 1, :] for j in range(CONV_A_WIDTH)]
    taps_c = [ccw_ref[j:j + 1, :] for j in range(CONV_C_WIDTH)]

    def conformer_conv_block(b, r0, after):
        taps = taps_a if after is None else [tap + _ordering_zero(after) for tap in taps_a]
        base = PAD_A - ctx_a + r0
        parts = []
        for res in range(SUBLANES):
            js = [j for j in range(CONV_A_WIDTH) if (base + j) % SUBLANES == res]
            if not js:
                continue
            first = base + js[0]
            win = ua_scr[b, first:first + (js[-1] - js[0]) + CONV_ROWS, :]
            part = None
            for j in js:
                off = j - js[0]
                term = win[off:off + CONV_ROWS, :] * taps[j]
                part = term if part is None else part + term
            parts.append(part)
        conv = parts[0]
        for part in parts[1:]:
            conv = conv + part
        conv = conv + cab
        mu = jnp.mean(conv, axis=-1, keepdims=True)
        cen = conv - mu
        var = jnp.mean(cen * cen, axis=-1, keepdims=True)
        y = cen * lax.rsqrt(var + EPS) * lng + lnb
        y = y * jax.nn.sigmoid(y)
        ya_ref[b, r0:r0 + CONV_ROWS, :] = _rms(y, ona).astype(BF16)

    def short_conv_block(b, r0):
        basec = PAD_C - ctx_c + r0
        convc = None
        for j in range(CONV_C_WIDTH):
            term = uc_scr[b, basec + j:basec + j + CONV_ROWS, :] * taps_c[j]
            convc = term if convc is None else convc + term
        row0 = b * ts + r0
        yc = held["g_b"][row0:row0 + CONV_ROWS, :] * convc
        yc_ref[b, r0:r0 + CONV_ROWS, :] = _rms(yc, onc).astype(BF16)

    blocks = [(b, r0) for b in range(bb) for r0 in range(0, ts, CONV_ROWS)]
    pending_c = list(blocks)
    after = None
    for n, blk in enumerate(blocks):
        conformer_conv_block(*blk, after)
        if n < len(stages):
            after = stages[n]()
        if n >= c_conv_lag - 1:
            short_conv_block(*pending_c.pop(0))
    for stage in stages[len(blocks):]:
        stage()
    for blk in pending_c:
        short_conv_block(*blk)


def _mixer_in(x, ctx_a, ctx_c, ctx_layer, layer_w, hsum, layer, *, bb, ts, name):
    nb, seq, d = x.shape
    d_conv = layer_w[2].shape[-1]
    d_sc = layer_w[8].shape[-1]
    d_att = hsum.shape[0]
    assert nb % bb == 0 and seq % ts == 0 and ts % CONV_ROWS == 0 and ts >= CONV_A_WIDTH - 1
    tail = min(BAND_PAST, seq)
    assert tail == ts, "the last sequence tile must hold exactly the cached band rows"
    grid = (nb // bb, seq // ts)

    def tok(w):
        return pl.BlockSpec((bb, ts, w), lambda b, t: (b, t, 0))

    def per_seq(rows, w):
        return pl.BlockSpec((bb, rows, w), lambda b, t: (b, 0, 0))

    def ctx_spec(a):
        if ctx_layer is None:
            return per_seq(a.shape[-2], a.shape[-1])
        return pl.BlockSpec((None, bb) + a.shape[2:], lambda b, t: (ctx_layer, b, 0, 0))

    kern = functools.partial(_mixer_in_kernel, bb=bb, ts=ts, d_conv=d_conv, d_att=d_att, d_sc=d_sc)
    tail_shape = (nb, tail, d_att)
    tail_spec = per_seq(tail, d_att)
    out_shape = (
        jax.ShapeDtypeStruct((nb, seq, d_conv), BF16),
        jax.ShapeDtypeStruct((nb, seq, d_sc), BF16),
        jax.ShapeDtypeStruct((nb, seq, d_att), BF16),
        jax.ShapeDtypeStruct((nb, seq, d_att), BF16),
        jax.ShapeDtypeStruct((nb, seq, d_att), BF16),
        jax.ShapeDtypeStruct((nb, CONV_A_WIDTH - 1, d_conv), F32),
        jax.ShapeDtypeStruct((nb, CONV_C_WIDTH - 1, d_sc), F32),
        jax.ShapeDtypeStruct(tail_shape, F32),
        jax.ShapeDtypeStruct(tail_shape, F32),
    )
    out_specs = (
        tok(d_conv), tok(d_sc), tok(d_att), tok(d_att), tok(d_att),
        per_seq(CONV_A_WIDTH - 1, d_conv), per_seq(CONV_C_WIDTH - 1, d_sc), tail_spec, tail_spec,
    )
    in_specs = ([tok(d), ctx_spec(ctx_a), ctx_spec(ctx_c)]
                + [_layer_spec(a, layer) for a in layer_w]
                + [pl.BlockSpec(hsum.shape, lambda b, t: (0, 0))])
    return pl.pallas_call(
        kern, grid=grid, in_specs=in_specs, out_specs=out_specs, out_shape=out_shape,
        scratch_shapes=[pltpu.VMEM((bb, PAD_A + ts, d_conv), F32),
                        pltpu.VMEM((bb, PAD_C + ts, d_sc), F32)],
        compiler_params=_params(2), name=name,
    )(x, ctx_a, ctx_c, *layer_w, hsum)


def _pair_consts():
    lane = lax.broadcasted_iota(jnp.int32, (1, LANES), 1)
    lo = lane < HEAD_DIM_B
    hi = lane >= HEAD_DIM_B
    keep_bf = (lo.astype(BF16), hi.astype(BF16))
    keep_f = (lo.astype(F32), hi.astype(F32))
    den_lane = (HEAD_DIM_B, 0)
    ones_bf = tuple((lane == dl).astype(BF16) for dl in den_lane)
    return keep_bf, keep_f, ones_bf, den_lane


def _probabilities(scores):
    mx = scores[0].max(axis=-1, keepdims=True)
    for sj in scores[1:]:
        mx = jnp.maximum(mx, sj.max(axis=-1, keepdims=True))
    return [jnp.exp2(sj - mx).astype(BF16) for sj in scores]


def _skewed_heads(n_heads, score, weighted_sum, probabilities=_probabilities):
    scores, probs, outs = {}, {}, []
    for i in range(n_heads + 2):
        if i < n_heads:
            scores[i] = score(i)
        if 0 <= i - 2 < n_heads:
            outs.append(weighted_sum(i - 2, probs.pop(i - 2)))
        if 0 <= i - 1 < n_heads:
            probs[i - 1] = probabilities(scores.pop(i - 1))
    return outs


def _fill_band_bias(bias_scr, gen_ref, tq, first_variant):
    variants, n_heads, _, nk = bias_scr.shape
    assert nk + tq - 1 <= BIAS_ROW
    qi = lax.broadcasted_iota(jnp.int32, (tq, nk), 0)
    km = lax.broadcasted_iota(jnp.int32, (tq, nk), 1)
    shift = CHUNK.bit_length() - 1
    dist = (qi >> shift) + BAND_CHUNKS - (km >> shift)
    in_band = (dist >= 0) & (dist <= BAND_CHUNKS)
    for h in range(n_heads):
        rows = jnp.broadcast_to(gen_ref[h], (tq, BIAS_ROW))
        toeplitz = pltpu.roll(rows, 0, 1, stride=1, stride_axis=0)[:, :nk]
        for v in range(variants):
            first_valid = BAND_PAST - (first_variant + v) * tq
            valid = in_band if first_valid <= 0 else in_band & (km >= first_valid)
            bias_scr[v, h] = jnp.where(valid, toeplitz, NEG)


def _pair_outputs(terms):
    return jnp.concatenate([terms[2 * p] + terms[2 * p + 1] for p in range(len(terms) // 2)],
                           axis=-1)


def _band_prompt_kernel(*refs, tq, nkb):
    q_ref = refs[0]
    k_refs = refs[1:1 + nkb]
    v_refs = refs[1 + nkb:1 + 2 * nkb]
    gen_ref, onb_ref, yb_ref, bias_scr = refs[1 + 2 * nkb:]
    keep_bf, keep_f, ones_bf, den_lane = _pair_consts()

    @pl.when((pl.program_id(0) == 0) & (pl.program_id(1) == 0))
    def _():
        _fill_band_bias(bias_scr, gen_ref, tq, 0)

    variant = jnp.minimum(pl.program_id(1), nkb - 1)

    def score(h):
        p, hh = divmod(h, 2)
        cols = slice(p * LANES, (p + 1) * LANES)
        qh = q_ref[0, :, cols] * keep_bf[hh]
        return [_dot_nt(qh, kr[0, :, cols]) + bias_scr[variant, h, :, j * tq:(j + 1) * tq]
                for j, kr in enumerate(k_refs)]

    def weighted_sum(h, probs):
        p, hh = divmod(h, 2)
        cols = slice(p * LANES, (p + 1) * LANES)
        acc = None
        for e, vr in zip(probs, v_refs):
            term = _dot(e, vr[0, :, cols] * keep_bf[hh] + ones_bf[hh])
            acc = term if acc is None else acc + term
        den = acc[:, den_lane[hh]:den_lane[hh] + 1]
        return acc * keep_f[hh] / den

    o = _pair_outputs(_skewed_heads(N_HEADS_B, score, weighted_sum))
    yb_ref[0] = _rms(o, onb_ref[...]).astype(BF16)


def _band_prompt(q, k, v, bias_gen, onb, layer, *, tq, name):
    nb, seq, d_att = q.shape
    assert BAND_PAST % tq == 0 and seq % tq == 0 and tq % CHUNK == 0 and tq <= MAX_TQ
    nkb = BAND_PAST // tq + 1
    grid = (nb, seq // tq)

    def kv_spec(j):
        back = nkb - 1 - j
        return pl.BlockSpec((1, tq, d_att), lambda b, t: (b, jnp.maximum(t - back, 0), 0))

    in_specs = ([pl.BlockSpec((1, tq, d_att), lambda b, t: (b, t, 0))]
                + [kv_spec(j) for j in range(nkb)] + [kv_spec(j) for j in range(nkb)]
                + [_layer_spec(bias_gen, layer), _layer_spec(onb, layer)])
    return pl.pallas_call(
        functools.partial(_band_prompt_kernel, tq=tq, nkb=nkb),
        grid=grid, in_specs=in_specs,
        out_specs=pl.BlockSpec((1, tq, d_att), lambda b, t: (b, t, 0)),
        out_shape=jax.ShapeDtypeStruct((nb, seq, d_att), BF16),
        scratch_shapes=[pltpu.VMEM((nkb, N_HEADS_B, tq, nkb * tq), F32)],
        compiler_params=_params(2), name=name,
    )(q, *([k] * nkb), *([v] * nkb), bias_gen, onb)


def _band_sample_kernel(q_ref, kn_ref, vn_ref, kf_ref, vf_ref, kct_ref, vct_ref, gen_ref, onb_ref,
                        yb_ref, knew_ref, vnew_ref, bias_scr, *, bb):
    ts = q_ref.shape[1]
    past = kct_ref.shape[-1]

    @pl.when(pl.program_id(0) == 0)
    def _():
        _fill_band_bias(bias_scr, gen_ref, ts, BAND_PAST // ts)

    bias_ref = bias_scr.at[0]
    keep_bf, keep_f, ones_bf, den_lane = _pair_consts()
    row = lax.broadcasted_iota(jnp.int32, (LANES, past), 0)
    keep_rows = ((row < HEAD_DIM_B).astype(BF16), (row >= HEAD_DIM_B).astype(BF16))
    ones_rows = tuple((row == dl).astype(BF16) for dl in den_lane)
    lane = lax.broadcasted_iota(jnp.int32, (1, LANES), 1)
    onb = onb_ref[...]

    def body(i, carry):
        kct = kct_ref[i]
        vct = vct_ref[i]

        def score(h):
            p, hh = divmod(h, 2)
            cols = slice(p * LANES, (p + 1) * LANES)
            qh = q_ref[i, :, cols] * keep_bf[hh]
            ktp = kct[2 * p:2 * p + 2].reshape(LANES, past).astype(BF16)
            return [_dot(qh, ktp) + bias_ref[h, :, :past],
                    _dot_nt(qh, kn_ref[i, :, cols]) + bias_ref[h, :, past:]]

        def weighted_sum(h, probs):
            p, hh = divmod(h, 2)
            cols = slice(p * LANES, (p + 1) * LANES)
            vtp = vct[2 * p:2 * p + 2].reshape(LANES, past).astype(BF16)
            acc = (_dot_nt(probs[0], vtp * keep_rows[hh] + ones_rows[hh])
                   + _dot(probs[1], vn_ref[i, :, cols] * keep_bf[hh] + ones_bf[hh]))
            den = acc[:, den_lane[hh]:den_lane[hh] + 1]
            return acc * keep_f[hh] / den

        o = _pair_outputs(_skewed_heads(N_HEADS_B, score, weighted_sum))
        yb_ref[i] = _rms(o, onb).astype(BF16)

        for old, new_ref, out_ref in ((kct, kf_ref, knew_ref), (vct, vf_ref, vnew_ref)):
            old2 = old.reshape(N_HEADS_B * HEAD_DIM_B, past)
            shifted = pltpu.roll(old2, past - ts, axis=1)
            pad_rows = jnp.zeros((LANES - ts, new_ref.shape[-1]), F32)
            new_t = jnp.concatenate([pad_rows, new_ref[i]], axis=0).T
            last = jnp.where(lane >= LANES - ts, new_t, shifted[:, past - LANES:])
            out_ref[i, :, :, :past - LANES] = shifted[:, :past - LANES].reshape(
                N_HEADS_B, HEAD_DIM_B, past - LANES)
            out_ref[i, :, :, past - LANES:] = last.reshape(N_HEADS_B, HEAD_DIM_B, LANES)
        return carry

    lax.fori_loop(0, bb, body, 0)


def _band_sample(q, kn, vn, kf, vf, kct, vct, bias_gen, onb, layer, *, bb, name):
    nb, ts, d_att = q.shape
    past = kct.shape[-1]
    assert nb % bb == 0 and ts < LANES and past >= 2 * LANES and past == BAND_PAST

    def seq_spec(rows):
        return pl.BlockSpec((bb, rows, d_att), lambda b: (b, 0, 0))

    cache_in = pl.BlockSpec((None, bb, N_HEADS_B, HEAD_DIM_B, past), lambda b: (layer, b, 0, 0, 0))
    cache_out = pl.BlockSpec((bb, N_HEADS_B, HEAD_DIM_B, past), lambda b: (b, 0, 0, 0))
    cache_shape = jax.ShapeDtypeStruct((nb, N_HEADS_B, HEAD_DIM_B, past), F32)
    in_specs = [seq_spec(ts)] * 5 + [cache_in, cache_in, _layer_spec(bias_gen, layer),
                                     _layer_spec(onb, layer)]
    return pl.pallas_call(
        functools.partial(_band_sample_kernel, bb=bb),
        grid=(nb // bb,), in_specs=in_specs, out_specs=(seq_spec(ts), cache_out, cache_out),
        out_shape=(jax.ShapeDtypeStruct((nb, ts, d_att), BF16), cache_shape, cache_shape),
        scratch_shapes=[pltpu.VMEM((1, N_HEADS_B, ts, past + ts), F32)],
        compiler_params=_params(1), name=name,
    )(q, kn, vn, kf, vf, kct, vct, bias_gen, onb)


def _bias_generator(rel_bias):
    depth, n_heads, n_rel = rel_bias.shape
    assert n_rel == 2 * REL_CLIP + 1 and MAX_TQ > REL_CLIP
    rb = rel_bias.astype(F32) * LOG2E
    far = rb[:, :, n_rel - 1:]
    pieces = [jnp.broadcast_to(far, (depth, n_heads, BAND_PAST - REL_CLIP)),
              rb[:, :, ::-1],
              jnp.broadcast_to(rb[:, :, :1], (depth, n_heads, MAX_TQ - REL_CLIP - 1)),
              jnp.broadcast_to(far, (depth, n_heads, BIAS_ROW - BAND_PAST - MAX_TQ))]
    row = jnp.concatenate(pieces, axis=-1)
    assert row.shape[-1] == BIAS_ROW
    return row[:, :, None, :]


def _post_kernel(x_ref, ya_ref, yb_ref, yc_ref, mk_ref, mv_ref, wout_ref, nx_ref, wxq_ref,
                 xqn_ref, wxo_ref, nff_ref, wup_ref, wdn_ref, o_ref, *, bb, ts, ff_chunk):
    m = bb * ts
    d = x_ref.shape[-1]
    x = x_ref[...].reshape(m, d)
    y = jnp.concatenate([ya_ref[...].reshape(m, -1), yb_ref[...].reshape(m, -1),
                         yc_ref[...].reshape(m, -1)], axis=-1)
    x1 = x + _dot(y, wout_ref[...])

    qx = _dot(_rms(x1, nx_ref[...]).astype(BF16), wxq_ref[...])
    xqn = xqn_ref[...]
    qhs = [(_rms(qx[:, h * HEAD_DIM_X:(h + 1) * HEAD_DIM_X], xqn)
            * (HEAD_DIM_X ** -0.5 * LOG2E)).astype(BF16) for h in range(N_HEADS_X)]

    def score(i):
        h, b = divmod(i, bb)
        return _dot_nt(qhs[h][b * ts:(b + 1) * ts], mk_ref[b, h])

    def softmax(s):
        e = jnp.exp2(s - s.max(axis=-1, keepdims=True))
        return e.astype(BF16), e.sum(axis=-1, keepdims=True)

    def weighted_sum(i, prob_den):
        h, b = divmod(i, bb)
        return _dot(prob_den[0], mv_ref[b, h]) / prob_den[1]

    outs = _skewed_heads(N_HEADS_X * bb, score, weighted_sum, softmax)
    heads = [outs[h * bb] if bb == 1 else jnp.concatenate(outs[h * bb:(h + 1) * bb], axis=0)
             for h in range(N_HEADS_X)]
    o = jnp.concatenate(heads, axis=-1).astype(BF16)
    x2 = x1 + _dot(o, wxo_ref[...])

    xn = _rms(x2, nff_ref[...]).astype(BF16)
    d_ff = wup_ref.shape[1]
    acc = x2
    for c0 in range(0, d_ff, ff_chunk):
        hcol = jnp.maximum(_dot(xn, wup_ref[:, c0:c0 + ff_chunk]), 0.0)
        acc = acc + _dot((hcol * hcol).astype(BF16), wdn_ref[c0:c0 + ff_chunk, :])
    o_ref[...] = acc.reshape(bb, ts, d)


def _post(x, ya, yb, yc, mk, mv, layer_w, layer, *, bb, ts, name):
    nb, seq, d = x.shape
    assert nb % bb == 0 and seq % ts == 0
    grid = (nb // bb, seq // ts)

    def tok(w):
        return pl.BlockSpec((bb, ts, w), lambda b, t: (b, t, 0))

    def weight(a):
        zeros = (0,) * (a.ndim - 1)
        return pl.BlockSpec((None,) + a.shape[1:], lambda b, t: (layer,) + zeros,
                            pipeline_mode=pl.Buffered(1))

    mem = pl.BlockSpec((None, bb) + mk.shape[2:], lambda b, t: (layer, b, 0, 0, 0))
    in_specs = ([tok(d), tok(ya.shape[-1]), tok(yb.shape[-1]), tok(yc.shape[-1]), mem, mem]
                + [weight(a) for a in layer_w])
    return pl.pallas_call(
        functools.partial(_post_kernel, bb=bb, ts=ts, ff_chunk=1024),
        grid=grid, in_specs=in_specs, out_specs=tok(d),
        out_shape=jax.ShapeDtypeStruct((nb, seq, d), F32),
        compiler_params=_params(2), name=name,
    )(x, ya, yb, yc, mk, mv, *layer_w)


def _memory_kv_kernel(mem_ref, nmem_ref, wxk_ref, wxv_ref, xkn_ref, mk_ref, mv_ref, mkh_ref,
                      mvh_ref):
    mn = _rms(mem_ref[0], nmem_ref[0]).astype(BF16)
    zk = _dot(mn, wxk_ref[0])
    zv = _dot(mn, wxv_ref[0])
    xkn = xkn_ref[0]
    heads = [_rms(zk[:, h * HEAD_DIM_X:(h + 1) * HEAD_DIM_X], xkn) for h in range(N_HEADS_X)]
    mk_ref[0, 0] = jnp.concatenate(heads, axis=-1)
    mv_ref[0, 0] = zv
    for h in range(N_HEADS_X):
        mkh_ref[0, 0, h] = heads[h].astype(BF16)
        mvh_ref[0, 0, h] = zv[:, h * HEAD_DIM_X:(h + 1) * HEAD_DIM_X].astype(BF16)


def _memory_kv(mem, nmem, wxk, wxv, xkn):
    depth, d, dk = wxk.shape
    nb, n_mem, _ = mem.shape
    out = jax.ShapeDtypeStruct((depth, nb, n_mem, dk), F32)
    out_h = jax.ShapeDtypeStruct((depth, nb, N_HEADS_X, n_mem, HEAD_DIM_X), BF16)
    out_spec = pl.BlockSpec((1, 1, n_mem, dk), lambda l, b: (l, b, 0, 0))
    out_h_spec = pl.BlockSpec((1, 1, N_HEADS_X, n_mem, HEAD_DIM_X), lambda l, b: (l, b, 0, 0, 0))
    return pl.pallas_call(
        _memory_kv_kernel, grid=(depth, nb),
        in_specs=[pl.BlockSpec((1, n_mem, d), lambda l, b: (b, 0, 0)),
                  pl.BlockSpec((1, 1, d), lambda l, b: (l, 0, 0)),
                  pl.BlockSpec((1, d, dk), lambda l, b: (l, 0, 0)),
                  pl.BlockSpec((1, d, dk), lambda l, b: (l, 0, 0)),
                  pl.BlockSpec((1, 1, HEAD_DIM_X), lambda l, b: (l, 0, 0))],
        out_specs=(out_spec, out_spec, out_h_spec, out_h_spec),
        out_shape=(out, out, out_h, out_h),
        compiler_params=_params(2), name="memory_kv",
    )(mem, nmem, wxk, wxv, xkn)


def _tile_config(nb, seq):
    ts = min(BAND_PAST, seq)
    bb = max(1, min(nb, BAND_PAST // ts))
    while nb % bb:
        bb -= 1
    return bb, ts


def kernel(x_prompt, x_sample, cache_conv_a, cache_band_k, cache_band_v, cache_conv_c, cache_mem_k, cache_mem_v, mem_prompt, norm_mix, w_in, conv_a_w, conv_a_b, ln_a_g, ln_a_b, q_norm, k_norm, rel_bias, conv_c_w, out_norm, w_out, norm_x, norm_mem, w_xq, w_xk, w_xv, xq_norm, xk_norm, w_xo, norm_ff, w_up, w_down):
    depth = w_in.shape[0]
    bp, seq_p, d = x_prompt.shape
    bs, seq_s, _ = x_sample.shape
    d_conv = conv_a_w.shape[-1]
    d_sc = conv_c_w.shape[-1]
    d_att = N_HEADS_B * HEAD_DIM_B
    past = cache_band_k.shape[2]
    assert past == BAND_PAST and seq_s == CHUNK, "sample step: one chunk behind a full band"

    bb_p, ts_p = _tile_config(bp, seq_p)
    bb_s, ts_s = _tile_config(bs, seq_s)
    tq_p = min(256, seq_p)
    bb_band_s = 4 if bs % 4 == 0 else 1

    row = lambda a: a[:, None, :]
    w_in_b, w_out_b, w_xq_b, w_xk_b, w_xv_b, w_xo_b, w_up_b, w_down_b = (
        a.astype(BF16) for a in (w_in, w_out, w_xq, w_xk, w_xv, w_xo, w_up, w_down))
    nmem, xkn = row(norm_mem), row(xk_norm)
    onb = row(out_norm[:, d_conv:d_conv + d_att])
    mix_w = (row(norm_mix), w_in_b, conv_a_w, row(conv_a_b), row(ln_a_g), row(ln_a_b),
             row(jnp.tile(q_norm, (1, N_HEADS_B))), row(jnp.tile(k_norm, (1, N_HEADS_B))),
             conv_c_w, row(out_norm[:, :d_conv]), row(out_norm[:, d_conv + d_att:]))
    post_w = (w_out_b, row(norm_x), w_xq_b, row(xq_norm), w_xo_b, row(norm_ff), w_up_b, w_down_b)
    head_id = jnp.arange(d_att) // HEAD_DIM_B
    hsum = (head_id[:, None] == head_id[None, :]).astype(BF16)

    bias_gen = _bias_generator(rel_bias)

    mk_p, mv_p, mk_ph, mv_ph = _memory_kv(mem_prompt, nmem, w_xk_b, w_xv_b, xkn)
    mk_sh = jnp.transpose(cache_mem_k, (0, 1, 3, 2, 4)).astype(BF16)
    mv_sh = jnp.transpose(cache_mem_v, (0, 1, 3, 2, 4)).astype(BF16)
    kct = jnp.transpose(cache_band_k, (0, 1, 3, 4, 2))
    vct = jnp.transpose(cache_band_v, (0, 1, 3, 4, 2))

    zero_a = jnp.zeros((bp, CONV_A_WIDTH - 1, d_conv), F32)
    zero_c = jnp.zeros((bp, CONV_C_WIDTH - 1, d_sc), F32)

    yp, ys = x_prompt, x_sample
    ca_p, kb_p, vb_p, cc_p = [], [], [], []
    ca_s, kb_s, vb_s, cc_s = [], [], [], []
    for l in range(depth):
        ya, yc, q, k, v, sa, sc, kf, vf = _mixer_in(
            yp, zero_a, zero_c, None, mix_w, hsum, l, bb=bb_p, ts=ts_p, name=f"mixer_in_p{l}")
        yb = _band_prompt(q, k, v, bias_gen, onb, l, tq=tq_p, name=f"band_p{l}")
        yp = _post(yp, ya, yb, yc, mk_ph, mv_ph, post_w, l, bb=bb_p, ts=ts_p, name=f"post_p{l}")
        ca_p.append(sa); cc_p.append(sc); kb_p.append(kf); vb_p.append(vf)

        ya, yc, q, k, v, sa, sc, kf, vf = _mixer_in(
            ys, cache_conv_a, cache_conv_c, l, mix_w, hsum, l, bb=bb_s, ts=ts_s,
            name=f"mixer_in_s{l}")
        yb, knew, vnew = _band_sample(q, k, v, kf, vf, kct, vct, bias_gen, onb, l, bb=bb_band_s,
                                      name=f"band_s{l}")
        ys = _post(ys, ya, yb, yc, mk_sh, mv_sh, post_w, l, bb=bb_s, ts=ts_s, name=f"post_s{l}")
        ca_s.append(sa); cc_s.append(sc); kb_s.append(knew); vb_s.append(vnew)

    def rows_major(parts):
        return jnp.transpose(jnp.stack(parts), (0, 1, 4, 2, 3))

    def split_heads(parts, nb):
        return jnp.stack(parts).reshape(depth, nb, -1, N_HEADS_B, HEAD_DIM_B)

    return (yp, ys,
            jnp.stack(ca_p), split_heads(kb_p, bp), split_heads(vb_p, bp), jnp.stack(cc_p),
            mk_p.reshape(depth, bp, -1, N_HEADS_X, HEAD_DIM_X),
            mv_p.reshape(depth, bp, -1, N_HEADS_X, HEAD_DIM_X),
            jnp.stack(ca_s), rows_major(kb_s), rows_major(vb_s), jnp.stack(cc_s))
```

```python
import functools
import math

import jax
import jax.numpy as jnp
from jax import lax
from jax.experimental import pallas as pl
from jax.experimental.pallas import tpu as pltpu

EPS = 1e-6
NEG = -1e30
LOG2E = math.log2(math.e)
CHUNK = 64
BAND_CHUNKS = 8
BAND_PAST = BAND_CHUNKS * CHUNK
REL_CLIP = 128
N_HEADS_B = 8
HEAD_DIM_B = 64
N_HEADS_X = 4
HEAD_DIM_X = 256
CONV_A_WIDTH = 31
CONV_C_WIDTH = 3

SUBLANES = 8
LANES = 128
PAD_A = 32
PAD_C = 8
CONV_ROWS = 64
MAX_TQ = 256
BIAS_ROW = 1024
VMEM_LIMIT = 56 * 1024 * 1024

F32 = jnp.float32
BF16 = jnp.bfloat16


def _rms(x, g):
    return x * lax.rsqrt(jnp.mean(x * x, axis=-1, keepdims=True) + EPS) * g


def _ordering_zero(x):
    bits = lax.bitcast_convert_type(x, jnp.uint32)
    return lax.bitcast_convert_type((bits >> 16) >> 16, F32)


def _dot(a, b):
    return jnp.dot(a, b, preferred_element_type=F32)


def _dot_nt(a, b):
    return lax.dot_general(a, b, (((1,), (1,)), ((), ())), preferred_element_type=F32)


def _layer_spec(a, layer):
    zeros = (0,) * (a.ndim - 1)
    return pl.BlockSpec((None,) + a.shape[1:], lambda *_: (layer,) + zeros)


def _params(n_axes):
    return pltpu.CompilerParams(dimension_semantics=("arbitrary",) * n_axes,
                                vmem_limit_bytes=VMEM_LIMIT)


def _mixer_in_kernel(x_ref, ctxa_ref, ctxc_ref, nmix_ref, win_ref, caw_ref, cab_ref, lng_ref,
                     lnb_ref, qn_ref, kn_ref, ccw_ref, ona_ref, onc_ref, hsum_ref,
                     ya_ref, yc_ref, q_ref, k_ref, v_ref, newa_ref, newc_ref, kf_ref, vf_ref,
                     ua_scr, uc_scr, *, bb, ts, d_conv, d_att, d_sc):
    t = pl.program_id(1)
    m = bb * ts
    ctx_a = CONV_A_WIDTH - 1
    ctx_c = CONV_C_WIDTH - 1

    @pl.when(t == 0)
    def _():
        ua_scr[:, PAD_A - ctx_a:PAD_A, :] = ctxa_ref[...]
        uc_scr[:, PAD_C - ctx_c:PAD_C, :] = ctxc_ref[...]

    @pl.when(t > 0)
    def _():
        ua_scr[:, PAD_A - ctx_a:PAD_A, :] = ua_scr[:, PAD_A + ts - ctx_a:PAD_A + ts, :]
        uc_scr[:, PAD_C - ctx_c:PAD_C, :] = uc_scr[:, PAD_C + ts - ctx_c:PAD_C + ts, :]

    x = x_ref[...].reshape(m, x_ref.shape[-1])
    xn = _rms(x, nmix_ref[...]).astype(BF16)

    c0 = 0
    c1 = 2 * d_conv
    c2 = c1 + 3 * d_att
    c3 = c2 + 3 * d_sc

    za = _dot(xn, win_ref[:, c0:c1])
    u_a = za[:, :d_conv] * jax.nn.sigmoid(za[:, d_conv:])
    ua_scr[:, PAD_A:PAD_A + ts, :] = u_a.reshape(bb, ts, d_conv)
    newa_ref[...] = ua_scr[:, PAD_A + ts - ctx_a:PAD_A + ts, :]

    held = {}

    def store_tail(ref, val):
        ref[...] = val.reshape(bb, ts, d_att)

    def head_rms(z, gain):
        ss = _dot((z * z).astype(BF16), hsum_ref[...]) * (1.0 / HEAD_DIM_B)
        return z * lax.rsqrt(ss + EPS) * gain

    def token(z):
        return z[0:1, :d_conv]

    def stage_c_conv_input():
        z = _dot(xn, win_ref[:, c2 + d_sc:c3])
        uc_scr[:, PAD_C:PAD_C + ts, :] = (z[:, :d_sc] * z[:, d_sc:]).reshape(bb, ts, d_sc)
        newc_ref[...] = uc_scr[:, PAD_C + ts - ctx_c:PAD_C + ts, :]
        return token(z)

    def stage_c_gate():
        held["g_b"] = _dot(xn, win_ref[:, c2:c2 + d_sc])
        return token(held["g_b"])

    def stage_q_proj():
        held["zq"] = _dot(xn, win_ref[:, c1:c1 + d_att])
        return token(held["zq"])

    def stage_q_norm():
        qh = head_rms(held.pop("zq"), qn_ref[...]) * (HEAD_DIM_B ** -0.5 * LOG2E)
        q_ref[...] = qh.astype(BF16).reshape(bb, ts, d_att)
        return token(qh)

    def stage_k_proj():
        held["zk"] = _dot(xn, win_ref[:, c1 + d_att:c1 + 2 * d_att])
        return token(held["zk"])

    def stage_k_norm():
        kh = head_rms(held.pop("zk"), kn_ref[...])
        k_ref[...] = kh.astype(BF16).reshape(bb, ts, d_att)
        store_tail(kf_ref, kh)
        return token(kh)

    def stage_v_proj():
        zv = _dot(xn, win_ref[:, c1 + 2 * d_att:c2])
        v_ref[...] = zv.astype(BF16).reshape(bb, ts, d_att)
        store_tail(vf_ref, zv)
        return token(zv)

    stages = [stage_c_conv_input, stage_c_gate, stage_q_proj, stage_q_norm, stage_k_proj,
              stage_k_norm, stage_v_proj]
    c_conv_lag = 2

    cab = cab_ref[...]
    lng = lng_ref[...]
    lnb = lnb_ref[...]
    ona = ona_ref[...]
    onc = onc_ref[...]
    taps_a = [caw_ref[j:j + 1, :] for j in range(CONV_A_WIDTH)]
    taps_c = [ccw_ref[j:j + 1, :] for j in range(CONV_C_WIDTH)]

    def conformer_conv_block(b, r0, after):
        taps = taps_a if after is None else [tap + _ordering_zero(after) for tap in taps_a]
        base = PAD_A - ctx_a + r0
        parts = []
        for res in range(SUBLANES):
            js = [j for j in range(CONV_A_WIDTH) if (base + j) % SUBLANES == res]
            if not js:
                continue
            first = base + js[0]
            win = ua_scr[b, first:first + (js[-1] - js[0]) + CONV_ROWS, :]
            part = None
            for j in js:
                off = j - js[0]
                term = win[off:off + CONV_ROWS, :] * taps[j]
                part = term if part is None else part + term
            parts.append(part)
        conv = parts[0]
        for part in parts[1:]:
            conv = conv + part
        conv = conv + cab
        mu = jnp.mean(conv, axis=-1, keepdims=True)
        cen = conv - mu
        var = jnp.mean(cen * cen, axis=-1, keepdims=True)
        y = cen * lax.rsqrt(var + EPS) * lng + lnb
        y = y * jax.nn.sigmoid(y)
        ya_ref[b, r0:r0 + CONV_ROWS, :] = _rms(y, ona).astype(BF16)

    def short_conv_block(b, r0):
        basec = PAD_C - ctx_c + r0
        convc = None
        for j in range(CONV_C_WIDTH):
            term = uc_scr[b, basec + j:basec + j + CONV_ROWS, :] * taps_c[j]
            convc = term if convc is None else convc + term
        row0 = b * ts + r0
        yc = held["g_b"][row0:row0 + CONV_ROWS, :] * convc
        yc_ref[b, r0:r0 + CONV_ROWS, :] = _rms(yc, onc).astype(BF16)

    blocks = [(b, r0) for b in range(bb) for r0 in range(0, ts, CONV_ROWS)]
    pending_c = list(blocks)
    after = None
    for n, blk in enumerate(blocks):
        conformer_conv_block(*blk, after)
        if n < len(stages):
            after = stages[n]()
        if n >= c_conv_lag - 1:
            short_conv_block(*pending_c.pop(0))
    for stage in stages[len(blocks):]:
        stage()
    for blk in pending_c:
        short_conv_block(*blk)


def _mixer_in(x, ctx_a, ctx_c, ctx_layer, layer_w, hsum, layer, *, bb, ts, name):
    nb, seq, d = x.shape
    d_conv = layer_w[2].shape[-1]
    d_sc = layer_w[8].shape[-1]
    d_att = hsum.shape[0]
    assert nb % bb == 0 and seq % ts == 0 and ts % CONV_ROWS == 0 and ts >= CONV_A_WIDTH - 1
    tail = min(BAND_PAST, seq)
    assert tail == ts, "the last sequence tile must hold exactly the cached band rows"
    grid = (nb // bb, seq // ts)

    def tok(w):
        return pl.BlockSpec((bb, ts, w), lambda b, t: (b, t, 0))

    def per_seq(rows, w):
        return pl.BlockSpec((bb, rows, w), lambda b, t: (b, 0, 0))

    def ctx_spec(a):
        if ctx_layer is None:
            return per_seq(a.shape[-2], a.shape[-1])
        return pl.BlockSpec((None, bb) + a.shape[2:], lambda b, t: (ctx_layer, b, 0, 0))

    kern = functools.partial(_mixer_in_kernel, bb=bb, ts=ts, d_conv=d_conv, d_att=d_att, d_sc=d_sc)
    tail_shape = (nb, tail, d_att)
    tail_spec = per_seq(tail, d_att)
    out_shape = (
        jax.ShapeDtypeStruct((nb, seq, d_conv), BF16),
        jax.ShapeDtypeStruct((nb, seq, d_sc), BF16),
        jax.ShapeDtypeStruct((nb, seq, d_att), BF16),
        jax.ShapeDtypeStruct((nb, seq, d_att), BF16),
        jax.ShapeDtypeStruct((nb, seq, d_att), BF16),
        jax.ShapeDtypeStruct((nb, CONV_A_WIDTH - 1, d_conv), F32),
        jax.ShapeDtypeStruct((nb, CONV_C_WIDTH - 1, d_sc), F32),
        jax.ShapeDtypeStruct(tail_shape, F32),
        jax.ShapeDtypeStruct(tail_shape, F32),
    )
    out_specs = (
        tok(d_conv), tok(d_sc), tok(d_att), tok(d_att), tok(d_att),
        per_seq(CONV_A_WIDTH - 1, d_conv), per_seq(CONV_C_WIDTH - 1, d_sc), tail_spec, tail_spec,
    )
    in_specs = ([tok(d), ctx_spec(ctx_a), ctx_spec(ctx_c)]
                + [_layer_spec(a, layer) for a in layer_w]
                + [pl.BlockSpec(hsum.shape, lambda b, t: (0, 0))])
    return pl.pallas_call(
        kern, grid=grid, in_specs=in_specs, out_specs=out_specs, out_shape=out_shape,
        scratch_shapes=[pltpu.VMEM((bb, PAD_A + ts, d_conv), F32),
                        pltpu.VMEM((bb, PAD_C + ts, d_sc), F32)],
        compiler_params=_params(2), name=name,
    )(x, ctx_a, ctx_c, *layer_w, hsum)


def _pair_consts():
    lane = lax.broadcasted_iota(jnp.int32, (1, LANES), 1)
    lo = lane < HEAD_DIM_B
    hi = lane >= HEAD_DIM_B
    keep_bf = (lo.astype(BF16), hi.astype(BF16))
    keep_f = (lo.astype(F32), hi.astype(F32))
    den_lane = (HEAD_DIM_B, 0)
    ones_bf = tuple((lane == dl).astype(BF16) for dl in den_lane)
    return keep_bf, keep_f, ones_bf, den_lane


def _probabilities(scores):
    mx = scores[0].max(axis=-1, keepdims=True)
    for sj in scores[1:]:
        mx = jnp.maximum(mx, sj.max(axis=-1, keepdims=True))
    return [jnp.exp2(sj - mx).astype(BF16) for sj in scores]


def _skewed_heads(n_heads, score, weighted_sum, probabilities=_probabilities):
    scores, probs, outs = {}, {}, []
    for i in range(n_heads + 2):
        if i < n_heads:
            scores[i] = score(i)
        if 0 <= i - 1 < n_heads:
            probs[i - 1] = probabilities(scores.pop(i - 1))
        if 0 <= i - 2 < n_heads:
            outs.append(weighted_sum(i - 2, probs.pop(i - 2)))
    return outs


def _fill_band_bias(bias_scr, gen_ref, tq, first_variant):
    variants, n_heads, _, nk = bias_scr.shape
    assert nk + tq - 1 <= BIAS_ROW
    qi = lax.broadcasted_iota(jnp.int32, (tq, nk), 0)
    km = lax.broadcasted_iota(jnp.int32, (tq, nk), 1)
    shift = CHUNK.bit_length() - 1
    dist = (qi >> shift) + BAND_CHUNKS - (km >> shift)
    in_band = (dist >= 0) & (dist <= BAND_CHUNKS)
    for h in range(n_heads):
        rows = jnp.broadcast_to(gen_ref[h], (tq, BIAS_ROW))
        toeplitz = pltpu.roll(rows, 0, 1, stride=1, stride_axis=0)[:, :nk]
        for v in range(variants):
            first_valid = BAND_PAST - (first_variant + v) * tq
            valid = in_band if first_valid <= 0 else in_band & (km >= first_valid)
            bias_scr[v, h] = jnp.where(valid, toeplitz, NEG)


def _pair_outputs(terms):
    return jnp.concatenate([terms[2 * p] + terms[2 * p + 1] for p in range(len(terms) // 2)],
                           axis=-1)


def _band_prompt_kernel(*refs, tq, nkb, sub):
    nkv = nkb - 1 + sub
    q_ref = refs[0]
    k_refs = refs[1:1 + nkv]
    v_refs = refs[1 + nkv:1 + 2 * nkv]
    gen_ref, onb_ref, yb_ref, bias_scr = refs[1 + 2 * nkv:]
    keep_bf, keep_f, ones_bf, den_lane = _pair_consts()

    @pl.when((pl.program_id(0) == 0) & (pl.program_id(1) == 0))
    def _():
        _fill_band_bias(bias_scr, gen_ref, tq, 0)

    variants = [jnp.minimum(pl.program_id(1) * sub + s, nkb - 1) for s in range(sub)]

    def score(i):
        s, h = divmod(i, N_HEADS_B)
        p, hh = divmod(h, 2)
        cols = slice(p * LANES, (p + 1) * LANES)
        qh = q_ref[0, s * tq:(s + 1) * tq, cols] * keep_bf[hh]
        return [_dot_nt(qh, k_refs[s + j][0, :, cols])
                + bias_scr[variants[s], h, :, j * tq:(j + 1) * tq] for j in range(nkb)]

    def weighted_sum(i, probs):
        s, h = divmod(i, N_HEADS_B)
        p, hh = divmod(h, 2)
        cols = slice(p * LANES, (p + 1) * LANES)
        acc = None
        for j, e in enumerate(probs):
            term = _dot(e, v_refs[s + j][0, :, cols] * keep_bf[hh] + ones_bf[hh])
            acc = term if acc is None else acc + term
        den = acc[:, den_lane[hh]:den_lane[hh] + 1]
        return acc * keep_f[hh] / den

    terms = _skewed_heads(sub * N_HEADS_B, score, weighted_sum)
    onb = onb_ref[...]
    for s in range(sub):
        o = _pair_outputs(terms[s * N_HEADS_B:(s + 1) * N_HEADS_B])
        yb_ref[0, s * tq:(s + 1) * tq, :] = _rms(o, onb).astype(BF16)


def _band_prompt(q, k, v, bias_gen, onb, layer, *, tq, sub, name):
    nb, seq, d_att = q.shape
    assert BAND_PAST % tq == 0 and seq % (sub * tq) == 0 and tq % CHUNK == 0 and tq <= MAX_TQ
    nkb = BAND_PAST // tq + 1
    nkv = nkb - 1 + sub
    grid = (nb, seq // (sub * tq))

    def kv_spec(j):
        back = nkb - 1 - j
        return pl.BlockSpec((1, tq, d_att), lambda b, t: (b, jnp.maximum(t * sub - back, 0), 0))

    tile = pl.BlockSpec((1, sub * tq, d_att), lambda b, t: (b, t, 0))
    in_specs = ([tile] + [kv_spec(j) for j in range(nkv)] + [kv_spec(j) for j in range(nkv)]
                + [_layer_spec(bias_gen, layer), _layer_spec(onb, layer)])
    return pl.pallas_call(
        functools.partial(_band_prompt_kernel, tq=tq, nkb=nkb, sub=sub),
        grid=grid, in_specs=in_specs, out_specs=tile,
        out_shape=jax.ShapeDtypeStruct((nb, seq, d_att), BF16),
        scratch_shapes=[pltpu.VMEM((nkb, N_HEADS_B, tq, nkb * tq), F32)],
        compiler_params=_params(2), name=name,
    )(q, *([k] * nkv), *([v] * nkv), bias_gen, onb)


def _band_sample_kernel(q_ref, kn_ref, vn_ref, kct_ref, vct_ref, gen_ref, onb_ref, yb_ref,
                        bias_scr, *, bb):
    ts = q_ref.shape[1]
    past = kct_ref.shape[-1]

    @pl.when(pl.program_id(0) == 0)
    def _():
        _fill_band_bias(bias_scr, gen_ref, ts, BAND_PAST // ts)

    bias_ref = bias_scr.at[0]
    keep_bf, keep_f, ones_bf, den_lane = _pair_consts()
    row = lax.broadcasted_iota(jnp.int32, (LANES, past), 0)
    keep_rows = ((row < HEAD_DIM_B).astype(BF16), (row >= HEAD_DIM_B).astype(BF16))
    ones_rows = tuple((row == dl).astype(BF16) for dl in den_lane)
    onb = onb_ref[...]

    def body(i, carry):
        kct = kct_ref[i]
        vct = vct_ref[i]

        def score(h):
            p, hh = divmod(h, 2)
            cols = slice(p * LANES, (p + 1) * LANES)
            qh = q_ref[i, :, cols] * keep_bf[hh]
            ktp = kct[2 * p:2 * p + 2].reshape(LANES, past).astype(BF16)
            return [_dot(qh, ktp) + bias_ref[h, :, :past],
                    _dot_nt(qh, kn_ref[i, :, cols]) + bias_ref[h, :, past:]]

        def weighted_sum(h, probs):
            p, hh = divmod(h, 2)
            cols = slice(p * LANES, (p + 1) * LANES)
            vtp = vct[2 * p:2 * p + 2].reshape(LANES, past).astype(BF16)
            acc = (_dot_nt(probs[0], vtp * keep_rows[hh] + ones_rows[hh])
                   + _dot(probs[1], vn_ref[i, :, cols] * keep_bf[hh] + ones_bf[hh]))
            den = acc[:, den_lane[hh]:den_lane[hh] + 1]
            return acc * keep_f[hh] / den

        o = _pair_outputs(_skewed_heads(N_HEADS_B, score, weighted_sum))
        yb_ref[i] = _rms(o, onb).astype(BF16)
        return carry

    lax.fori_loop(0, bb, body, 0)


def _band_sample(q, kn, vn, kct, vct, bias_gen, onb, layer, *, bb, name):
    nb, ts, d_att = q.shape
    past = kct.shape[-1]
    assert nb % bb == 0 and past == BAND_PAST

    def seq_spec(rows):
        return pl.BlockSpec((bb, rows, d_att), lambda b: (b, 0, 0))

    cache_in = pl.BlockSpec((None, bb, N_HEADS_B, HEAD_DIM_B, past), lambda b: (layer, b, 0, 0, 0))
    in_specs = [seq_spec(ts)] * 3 + [cache_in, cache_in, _layer_spec(bias_gen, layer),
                                     _layer_spec(onb, layer)]
    return pl.pallas_call(
        functools.partial(_band_sample_kernel, bb=bb),
        grid=(nb // bb,), in_specs=in_specs, out_specs=seq_spec(ts),
        out_shape=jax.ShapeDtypeStruct((nb, ts, d_att), BF16),
        scratch_shapes=[pltpu.VMEM((1, N_HEADS_B, ts, past + ts), F32)],
        compiler_params=_params(1), name=name,
    )(q, kn, vn, kct, vct, bias_gen, onb)


def _band_cache_update_kernel(kct_ref, vct_ref, kf_ref, vf_ref, knew_ref, vnew_ref, *, bb):
    ts = kf_ref.shape[1]
    past = kct_ref.shape[-1]
    lane = lax.broadcasted_iota(jnp.int32, (1, LANES), 1)

    def body(i, carry):
        for old_ref, new_ref, out_ref in ((kct_ref, kf_ref, knew_ref), (vct_ref, vf_ref, vnew_ref)):
            old2 = old_ref[i].reshape(N_HEADS_B * HEAD_DIM_B, past)
            shifted = pltpu.roll(old2, past - ts, axis=1)
            pad_rows = jnp.zeros((LANES - ts, new_ref.shape[-1]), F32)
            new_t = jnp.concatenate([pad_rows, new_ref[i]], axis=0).T
            last = jnp.where(lane >= LANES - ts, new_t, shifted[:, past - LANES:])
            out_ref[i, :, :, :past - LANES] = shifted[:, :past - LANES].reshape(
                N_HEADS_B, HEAD_DIM_B, past - LANES)
            out_ref[i, :, :, past - LANES:] = last.reshape(N_HEADS_B, HEAD_DIM_B, LANES)
        return carry

    lax.fori_loop(0, bb, body, 0)


def _band_cache_update(kct, vct, kf, vf, *, bb):
    depth, nb, n_heads, hd, past = kct.shape
    ts, d_att = kf.shape[2:]
    assert nb % bb == 0 and ts < LANES and past >= 2 * LANES and d_att == n_heads * hd
    cache = pl.BlockSpec((None, bb, n_heads, hd, past), lambda l, b: (l, b, 0, 0, 0))
    rows = pl.BlockSpec((None, bb, ts, d_att), lambda l, b: (l, b, 0, 0))
    shape = jax.ShapeDtypeStruct(kct.shape, F32)
    return pl.pallas_call(
        functools.partial(_band_cache_update_kernel, bb=bb),
        grid=(depth, nb // bb), in_specs=[cache, cache, rows, rows], out_specs=(cache, cache),
        out_shape=(shape, shape), compiler_params=_params(2), name="band_cache_update",
    )(kct, vct, kf, vf)


def _bias_generator(rel_bias):
    depth, n_heads, n_rel = rel_bias.shape
    assert n_rel == 2 * REL_CLIP + 1 and MAX_TQ > REL_CLIP
    rb = rel_bias.astype(F32) * LOG2E
    far = rb[:, :, n_rel - 1:]
    pieces = [jnp.broadcast_to(far, (depth, n_heads, BAND_PAST - REL_CLIP)),
              rb[:, :, ::-1],
              jnp.broadcast_to(rb[:, :, :1], (depth, n_heads, MAX_TQ - REL_CLIP - 1)),
              jnp.broadcast_to(far, (depth, n_heads, BIAS_ROW - BAND_PAST - MAX_TQ))]
    row = jnp.concatenate(pieces, axis=-1)
    assert row.shape[-1] == BIAS_ROW
    return row[:, :, None, :]


def _post_kernel(x_ref, ya_ref, yb_ref, yc_ref, mk_ref, mv_ref, wout_ref, nx_ref, wxq_ref,
                 xqn_ref, wxo_ref, nff_ref, wup_ref, wdn_ref, o_ref, *, bb, ts, ff_chunk):
    m = bb * ts
    d = x_ref.shape[-1]
    x = x_ref[...].reshape(m, d)
    y = jnp.concatenate([ya_ref[...].reshape(m, -1), yb_ref[...].reshape(m, -1),
                         yc_ref[...].reshape(m, -1)], axis=-1)
    x1 = x + _dot(y, wout_ref[...])

    qx = _dot(_rms(x1, nx_ref[...]).astype(BF16), wxq_ref[...])
    xqn = xqn_ref[...]
    qhs = [(_rms(qx[:, h * HEAD_DIM_X:(h + 1) * HEAD_DIM_X], xqn)
            * (HEAD_DIM_X ** -0.5 * LOG2E)).astype(BF16) for h in range(N_HEADS_X)]

    def score(i):
        h, b = divmod(i, bb)
        return _dot_nt(qhs[h][b * ts:(b + 1) * ts], mk_ref[b, h])

    def softmax(s):
        e = jnp.exp2(s - s.max(axis=-1, keepdims=True))
        return e.astype(BF16), e.sum(axis=-1, keepdims=True)

    def weighted_sum(i, prob_den):
        h, b = divmod(i, bb)
        return _dot(prob_den[0], mv_ref[b, h]) / prob_den[1]

    outs = _skewed_heads(N_HEADS_X * bb, score, weighted_sum, softmax)
    heads = [outs[h * bb] if bb == 1 else jnp.concatenate(outs[h * bb:(h + 1) * bb], axis=0)
             for h in range(N_HEADS_X)]
    o = jnp.concatenate(heads, axis=-1).astype(BF16)
    x2 = x1 + _dot(o, wxo_ref[...])

    xn = _rms(x2, nff_ref[...]).astype(BF16)
    d_ff = wup_ref.shape[1]
    acc = x2
    for c0 in range(0, d_ff, ff_chunk):
        hcol = jnp.maximum(_dot(xn, wup_ref[:, c0:c0 + ff_chunk]), 0.0)
        acc = acc + _dot((hcol * hcol).astype(BF16), wdn_ref[c0:c0 + ff_chunk, :])
    o_ref[...] = acc.reshape(bb, ts, d)


def _post(x, ya, yb, yc, mk, mv, layer_w, layer, *, bb, ts, name):
    nb, seq, d = x.shape
    assert nb % bb == 0 and seq % ts == 0
    grid = (nb // bb, seq // ts)

    def tok(w):
        return pl.BlockSpec((bb, ts, w), lambda b, t: (b, t, 0))

    def weight(a):
        zeros = (0,) * (a.ndim - 1)
        return pl.BlockSpec((None,) + a.shape[1:], lambda b, t: (layer,) + zeros,
                            pipeline_mode=pl.Buffered(1))

    mem = pl.BlockSpec((None, bb) + mk.shape[2:], lambda b, t: (layer, b, 0, 0, 0))
    in_specs = ([tok(d), tok(ya.shape[-1]), tok(yb.shape[-1]), tok(yc.shape[-1]), mem, mem]
                + [weight(a) for a in layer_w])
    return pl.pallas_call(
        functools.partial(_post_kernel, bb=bb, ts=ts, ff_chunk=1024),
        grid=grid, in_specs=in_specs, out_specs=tok(d),
        out_shape=jax.ShapeDtypeStruct((nb, seq, d), F32),
        compiler_params=_params(2), name=name,
    )(x, ya, yb, yc, mk, mv, *layer_w)


def _memory_kv_kernel(mem_ref, nmem_ref, wxk_ref, wxv_ref, xkn_ref, mk_ref, mv_ref, mkh_ref,
                      mvh_ref):
    mn = _rms(mem_ref[0], nmem_ref[0]).astype(BF16)
    zk = _dot(mn, wxk_ref[0])
    zv = _dot(mn, wxv_ref[0])
    xkn = xkn_ref[0]
    heads = [_rms(zk[:, h * HEAD_DIM_X:(h + 1) * HEAD_DIM_X], xkn) for h in range(N_HEADS_X)]
    mk_ref[0, 0] = jnp.concatenate(heads, axis=-1)
    mv_ref[0, 0] = zv
    for h in range(N_HEADS_X):
        mkh_ref[0, 0, h] = heads[h].astype(BF16)
        mvh_ref[0, 0, h] = zv[:, h * HEAD_DIM_X:(h + 1) * HEAD_DIM_X].astype(BF16)


def _memory_kv(mem, nmem, wxk, wxv, xkn):
    depth, d, dk = wxk.shape
    nb, n_mem, _ = mem.shape
    out = jax.ShapeDtypeStruct((depth, nb, n_mem, dk), F32)
    out_h = jax.ShapeDtypeStruct((depth, nb, N_HEADS_X, n_mem, HEAD_DIM_X), BF16)
    out_spec = pl.BlockSpec((1, 1, n_mem, dk), lambda l, b: (l, b, 0, 0))
    out_h_spec = pl.BlockSpec((1, 1, N_HEADS_X, n_mem, HEAD_DIM_X), lambda l, b: (l, b, 0, 0, 0))
    return pl.pallas_call(
        _memory_kv_kernel, grid=(depth, nb),
        in_specs=[pl.BlockSpec((1, n_mem, d), lambda l, b: (b, 0, 0)),
                  pl.BlockSpec((1, 1, d), lambda l, b: (l, 0, 0)),
                  pl.BlockSpec((1, d, dk), lambda l, b: (l, 0, 0)),
                  pl.BlockSpec((1, d, dk), lambda l, b: (l, 0, 0)),
                  pl.BlockSpec((1, 1, HEAD_DIM_X), lambda l, b: (l, 0, 0))],
        out_specs=(out_spec, out_spec, out_h_spec, out_h_spec),
        out_shape=(out, out, out_h, out_h),
        compiler_params=_params(2), name="memory_kv",
    )(mem, nmem, wxk, wxv, xkn)


def _tile_config(nb, seq):
    ts = min(BAND_PAST, seq)
    bb = max(1, min(nb, BAND_PAST // ts))
    while nb % bb:
        bb -= 1
    return bb, ts


def kernel(x_prompt, x_sample, cache_conv_a, cache_band_k, cache_band_v, cache_conv_c, cache_mem_k, cache_mem_v, mem_prompt, norm_mix, w_in, conv_a_w, conv_a_b, ln_a_g, ln_a_b, q_norm, k_norm, rel_bias, conv_c_w, out_norm, w_out, norm_x, norm_mem, w_xq, w_xk, w_xv, xq_norm, xk_norm, w_xo, norm_ff, w_up, w_down):
    depth = w_in.shape[0]
    bp, seq_p, d = x_prompt.shape
    bs, seq_s, _ = x_sample.shape
    d_conv = conv_a_w.shape[-1]
    d_sc = conv_c_w.shape[-1]
    d_att = N_HEADS_B * HEAD_DIM_B
    past = cache_band_k.shape[2]
    assert past == BAND_PAST and seq_s == CHUNK, "sample step: one chunk behind a full band"

    bb_p, ts_p = _tile_config(bp, seq_p)
    bb_s, ts_s = _tile_config(bs, seq_s)
    tq_p = min(MAX_TQ, seq_p)
    sub_p = 2 if seq_p % (2 * tq_p) == 0 else 1
    bb_band_s = 4 if bs % 4 == 0 else 1

    row = lambda a: a[:, None, :]
    w_in_b, w_out_b, w_xq_b, w_xk_b, w_xv_b, w_xo_b, w_up_b, w_down_b = (
        a.astype(BF16) for a in (w_in, w_out, w_xq, w_xk, w_xv, w_xo, w_up, w_down))
    nmem, xkn = row(norm_mem), row(xk_norm)
    onb = row(out_norm[:, d_conv:d_conv + d_att])
    mix_w = (row(norm_mix), w_in_b, conv_a_w, row(conv_a_b), row(ln_a_g), row(ln_a_b),
             row(jnp.tile(q_norm, (1, N_HEADS_B))), row(jnp.tile(k_norm, (1, N_HEADS_B))),
             conv_c_w, row(out_norm[:, :d_conv]), row(out_norm[:, d_conv + d_att:]))
    post_w = (w_out_b, row(norm_x), w_xq_b, row(xq_norm), w_xo_b, row(norm_ff), w_up_b, w_down_b)
    head_id = jnp.arange(d_att) // HEAD_DIM_B
    hsum = (head_id[:, None] == head_id[None, :]).astype(BF16)

    bias_gen = _bias_generator(rel_bias)

    mk_p, mv_p, mk_ph, mv_ph = _memory_kv(mem_prompt, nmem, w_xk_b, w_xv_b, xkn)
    mk_sh = jnp.transpose(cache_mem_k, (0, 1, 3, 2, 4)).astype(BF16)
    mv_sh = jnp.transpose(cache_mem_v, (0, 1, 3, 2, 4)).astype(BF16)
    kct = jnp.transpose(cache_band_k, (0, 1, 3, 4, 2))
    vct = jnp.transpose(cache_band_v, (0, 1, 3, 4, 2))

    zero_a = jnp.zeros((bp, CONV_A_WIDTH - 1, d_conv), F32)
    zero_c = jnp.zeros((bp, CONV_C_WIDTH - 1, d_sc), F32)

    yp, ys = x_prompt, x_sample
    ca_p, kb_p, vb_p, cc_p = [], [], [], []
    ca_s, kb_s, vb_s, cc_s = [], [], [], []
    for l in range(depth):
        ya, yc, q, k, v, sa, sc, kf, vf = _mixer_in(
            yp, zero_a, zero_c, None, mix_w, hsum, l, bb=bb_p, ts=ts_p, name=f"mixer_in_p{l}")
        yb = _band_prompt(q, k, v, bias_gen, onb, l, tq=tq_p, sub=sub_p, name=f"band_p{l}")
        yp = _post(yp, ya, yb, yc, mk_ph, mv_ph, post_w, l, bb=bb_p, ts=ts_p, name=f"post_p{l}")
        ca_p.append(sa); cc_p.append(sc); kb_p.append(kf); vb_p.append(vf)

        ya, yc, q, k, v, sa, sc, kf, vf = _mixer_in(
            ys, cache_conv_a, cache_conv_c, l, mix_w, hsum, l, bb=bb_s, ts=ts_s,
            name=f"mixer_in_s{l}")
        yb = _band_sample(q, k, v, kct, vct, bias_gen, onb, l, bb=bb_band_s, name=f"band_s{l}")
        ys = _post(ys, ya, yb, yc, mk_sh, mv_sh, post_w, l, bb=bb_s, ts=ts_s, name=f"post_s{l}")
        ca_s.append(sa); cc_s.append(sc); kb_s.append(kf); vb_s.append(vf)

    knew, vnew = _band_cache_update(kct, vct, jnp.stack(kb_s), jnp.stack(vb_s), bb=bb_band_s)

    def rows_major(cache_t):
        return jnp.transpose(cache_t, (0, 1, 4, 2, 3))

    def split_heads(parts, nb):
        return jnp.stack(parts).reshape(depth, nb, -1, N_HEADS_B, HEAD_DIM_B)

    return (yp, ys,
            jnp.stack(ca_p), split_heads(kb_p, bp), split_heads(vb_p, bp), jnp.stack(cc_p),
            mk_p.reshape(depth, bp, -1, N_HEADS_X, HEAD_DIM_X),
            mv_p.reshape(depth, bp, -1, N_HEADS_X, HEAD_DIM_X),
            jnp.stack(ca_s), rows_major(knew), rows_major(vnew), jnp.stack(cc_s))
```

```python
import functools
import math

import jax
import jax.numpy as jnp
from jax import lax
from jax.experimental import pallas as pl
from jax.experimental.pallas import tpu as pltpu

EPS = 1e-6
NEG = -1e30
LOG2E = math.log2(math.e)
CHUNK = 64
BAND_CHUNKS = 8
BAND_PAST = BAND_CHUNKS * CHUNK
REL_CLIP = 128
N_HEADS_B = 8
HEAD_DIM_B = 64
N_HEADS_X = 4
HEAD_DIM_X = 256
CONV_A_WIDTH = 31
CONV_C_WIDTH = 3

SUBLANES = 8
LANES = 128
PAD_A = 32
PAD_C = 8
CONV_ROWS = 64
MAX_TQ = 256
BIAS_ROW = 1024
VMEM_LIMIT = 56 * 1024 * 1024

F32 = jnp.float32
BF16 = jnp.bfloat16


def _rms(x, g):
    return x * lax.rsqrt(jnp.mean(x * x, axis=-1, keepdims=True) + EPS) * g


def _ordering_zero(x):
    bits = lax.bitcast_convert_type(x, jnp.uint32)
    return lax.bitcast_convert_type((bits >> 16) >> 16, F32)


def _dot(a, b):
    return jnp.dot(a, b, preferred_element_type=F32)


def _dot_nt(a, b):
    return lax.dot_general(a, b, (((1,), (1,)), ((), ())), preferred_element_type=F32)


def _layer_spec(a, layer, single_buffer=False):
    zeros = (0,) * (a.ndim - 1)
    mode = dict(pipeline_mode=pl.Buffered(1)) if single_buffer else {}
    return pl.BlockSpec((None,) + a.shape[1:], lambda *_: (layer,) + zeros, **mode)


def _params(n_axes):
    return pltpu.CompilerParams(dimension_semantics=("arbitrary",) * n_axes,
                                vmem_limit_bytes=VMEM_LIMIT)


def _mixer_in_kernel(x_ref, ctxa_ref, ctxc_ref, nmix_ref, win_ref, caw_ref, cab_ref, lng_ref,
                     lnb_ref, qn_ref, kn_ref, ccw_ref, ona_ref, onc_ref, hsum_ref,
                     ya_ref, yc_ref, q_ref, k_ref, v_ref, newa_ref, newc_ref, kf_ref, vf_ref,
                     ua_scr, uc_scr, *, bb, ts, d_conv, d_att, d_sc):
    t = pl.program_id(1)
    m = bb * ts
    ctx_a = CONV_A_WIDTH - 1
    ctx_c = CONV_C_WIDTH - 1

    @pl.when(t == 0)
    def _():
        ua_scr[:, PAD_A - ctx_a:PAD_A, :] = ctxa_ref[...]
        uc_scr[:, PAD_C - ctx_c:PAD_C, :] = ctxc_ref[...]

    @pl.when(t > 0)
    def _():
        ua_scr[:, PAD_A - ctx_a:PAD_A, :] = ua_scr[:, PAD_A + ts - ctx_a:PAD_A + ts, :]
        uc_scr[:, PAD_C - ctx_c:PAD_C, :] = uc_scr[:, PAD_C + ts - ctx_c:PAD_C + ts, :]

    x = x_ref[...].reshape(m, x_ref.shape[-1])
    xn = _rms(x, nmix_ref[...]).astype(BF16)

    c0 = 0
    c1 = 2 * d_conv
    c2 = c1 + 3 * d_att
    c3 = c2 + 3 * d_sc

    za = _dot(xn, win_ref[:, c0:c1])
    u_a = za[:, :d_conv] * jax.nn.sigmoid(za[:, d_conv:])
    ua_scr[:, PAD_A:PAD_A + ts, :] = u_a.reshape(bb, ts, d_conv)
    newa_ref[...] = ua_scr[:, PAD_A + ts - ctx_a:PAD_A + ts, :]

    held = {}

    def store_tail(ref, val):
        tail = ref.shape[1]
        ref[...] = val.reshape(bb, ts, d_att)[:, ts - tail:, :]

    def head_rms(z, gain):
        ss = _dot((z * z).astype(BF16), hsum_ref[...]) * (1.0 / HEAD_DIM_B)
        return z * lax.rsqrt(ss + EPS) * gain

    def token(z):
        return z[0:1, :d_conv]

    def stage_c_conv_input():
        z = _dot(xn, win_ref[:, c2 + d_sc:c3])
        uc_scr[:, PAD_C:PAD_C + ts, :] = (z[:, :d_sc] * z[:, d_sc:]).reshape(bb, ts, d_sc)
        newc_ref[...] = uc_scr[:, PAD_C + ts - ctx_c:PAD_C + ts, :]
        return token(z)

    def stage_c_gate():
        held["g_b"] = _dot(xn, win_ref[:, c2:c2 + d_sc])
        return token(held["g_b"])

    def stage_q_proj():
        held["zq"] = _dot(xn, win_ref[:, c1:c1 + d_att])
        return token(held["zq"])

    def stage_q_norm():
        qh = head_rms(held.pop("zq"), qn_ref[...]) * (HEAD_DIM_B ** -0.5 * LOG2E)
        q_ref[...] = qh.astype(BF16).reshape(bb, ts, d_att)
        return token(qh)

    def stage_k_proj():
        held["zk"] = _dot(xn, win_ref[:, c1 + d_att:c1 + 2 * d_att])
        return token(held["zk"])

    def stage_k_norm():
        kh = head_rms(held.pop("zk"), kn_ref[...])
        k_ref[...] = kh.astype(BF16).reshape(bb, ts, d_att)
        store_tail(kf_ref, kh)
        return token(kh)

    def stage_v_proj():
        zv = _dot(xn, win_ref[:, c1 + 2 * d_att:c2])
        v_ref[...] = zv.astype(BF16).reshape(bb, ts, d_att)
        store_tail(vf_ref, zv)
        return token(zv)

    stages = [stage_c_conv_input, stage_c_gate, stage_q_proj, stage_q_norm, stage_k_proj,
              stage_k_norm, stage_v_proj]
    c_conv_lag = 2

    cab = cab_ref[...]
    lng = lng_ref[...]
    lnb = lnb_ref[...]
    ona = ona_ref[...]
    onc = onc_ref[...]
    taps_a = [caw_ref[j:j + 1, :] for j in range(CONV_A_WIDTH)]
    taps_c = [ccw_ref[j:j + 1, :] for j in range(CONV_C_WIDTH)]

    def conformer_conv_block(b, r0, after):
        taps = taps_a if after is None else [tap + _ordering_zero(after) for tap in taps_a]
        base = PAD_A - ctx_a + r0
        parts = []
        for res in range(SUBLANES):
            js = [j for j in range(CONV_A_WIDTH) if (base + j) % SUBLANES == res]
            if not js:
                continue
            first = base + js[0]
            win = ua_scr[b, first:first + (js[-1] - js[0]) + CONV_ROWS, :]
            part = None
            for j in js:
                off = j - js[0]
                term = win[off:off + CONV_ROWS, :] * taps[j]
                part = term if part is None else part + term
            parts.append(part)
        conv = parts[0]
        for part in parts[1:]:
            conv = conv + part
        conv = conv + cab
        mu = jnp.mean(conv, axis=-1, keepdims=True)
        cen = conv - mu
        var = jnp.mean(cen * cen, axis=-1, keepdims=True)
        y = cen * lax.rsqrt(var + EPS) * lng + lnb
        y = y * jax.nn.sigmoid(y)
        ya_ref[b, r0:r0 + CONV_ROWS, :] = _rms(y, ona).astype(BF16)

    def short_conv_block(b, r0):
        basec = PAD_C - ctx_c + r0
        convc = None
        for j in range(CONV_C_WIDTH):
            term = uc_scr[b, basec + j:basec + j + CONV_ROWS, :] * taps_c[j]
            convc = term if convc is None else convc + term
        row0 = b * ts + r0
        yc = held["g_b"][row0:row0 + CONV_ROWS, :] * convc
        yc_ref[b, r0:r0 + CONV_ROWS, :] = _rms(yc, onc).astype(BF16)

    blocks = [(b, r0) for b in range(bb) for r0 in range(0, ts, CONV_ROWS)]
    pending_c = list(blocks)
    after = None
    for n, blk in enumerate(blocks):
        conformer_conv_block(*blk, after)
        if n < len(stages):
            after = stages[n]()
        if n >= c_conv_lag - 1:
            short_conv_block(*pending_c.pop(0))
    for stage in stages[len(blocks):]:
        stage()
    for blk in pending_c:
        short_conv_block(*blk)


def _mixer_in(x, ctx_a, ctx_c, ctx_layer, layer_w, hsum, layer, *, bb, ts, name):
    nb, seq, d = x.shape
    d_conv = layer_w[2].shape[-1]
    d_sc = layer_w[8].shape[-1]
    d_att = hsum.shape[0]
    assert nb % bb == 0 and seq % ts == 0 and ts % CONV_ROWS == 0 and ts >= CONV_A_WIDTH - 1
    tail = min(BAND_PAST, seq)
    assert tail <= ts and tail % SUBLANES == 0, "the last tile must hold all cached band rows"
    grid = (nb // bb, seq // ts)

    def tok(w):
        return pl.BlockSpec((bb, ts, w), lambda b, t: (b, t, 0))

    def per_seq(rows, w):
        return pl.BlockSpec((bb, rows, w), lambda b, t: (b, 0, 0))

    def ctx_spec(a):
        if ctx_layer is None:
            return per_seq(a.shape[-2], a.shape[-1])
        return pl.BlockSpec((None, bb) + a.shape[2:], lambda b, t: (ctx_layer, b, 0, 0))

    kern = functools.partial(_mixer_in_kernel, bb=bb, ts=ts, d_conv=d_conv, d_att=d_att, d_sc=d_sc)
    tail_shape = (nb, tail, d_att)
    tail_spec = per_seq(tail, d_att)
    out_shape = (
        jax.ShapeDtypeStruct((nb, seq, d_conv), BF16),
        jax.ShapeDtypeStruct((nb, seq, d_sc), BF16),
        jax.ShapeDtypeStruct((nb, seq, d_att), BF16),
        jax.ShapeDtypeStruct((nb, seq, d_att), BF16),
        jax.ShapeDtypeStruct((nb, seq, d_att), BF16),
        jax.ShapeDtypeStruct((nb, CONV_A_WIDTH - 1, d_conv), F32),
        jax.ShapeDtypeStruct((nb, CONV_C_WIDTH - 1, d_sc), F32),
        jax.ShapeDtypeStruct(tail_shape, F32),
        jax.ShapeDtypeStruct(tail_shape, F32),
    )
    out_specs = (
        tok(d_conv), tok(d_sc), tok(d_att), tok(d_att), tok(d_att),
        per_seq(CONV_A_WIDTH - 1, d_conv), per_seq(CONV_C_WIDTH - 1, d_sc), tail_spec, tail_spec,
    )
    in_specs = ([tok(d), ctx_spec(ctx_a), ctx_spec(ctx_c)]
                + [_layer_spec(a, layer, single_buffer=True) for a in layer_w]
                + [pl.BlockSpec(hsum.shape, lambda b, t: (0, 0), pipeline_mode=pl.Buffered(1))])
    return pl.pallas_call(
        kern, grid=grid, in_specs=in_specs, out_specs=out_specs, out_shape=out_shape,
        scratch_shapes=[pltpu.VMEM((bb, PAD_A + ts, d_conv), F32),
                        pltpu.VMEM((bb, PAD_C + ts, d_sc), F32)],
        compiler_params=_params(2), name=name,
    )(x, ctx_a, ctx_c, *layer_w, hsum)


def _pair_consts():
    lane = lax.broadcasted_iota(jnp.int32, (1, LANES), 1)
    lo = lane < HEAD_DIM_B
    hi = lane >= HEAD_DIM_B
    keep_bf = (lo.astype(BF16), hi.astype(BF16))
    keep_f = (lo.astype(F32), hi.astype(F32))
    den_lane = (HEAD_DIM_B, 0)
    ones_bf = tuple((lane == dl).astype(BF16) for dl in den_lane)
    return keep_bf, keep_f, ones_bf, den_lane


def _probabilities(scores):
    mx = scores[0].max(axis=-1, keepdims=True)
    for sj in scores[1:]:
        mx = jnp.maximum(mx, sj.max(axis=-1, keepdims=True))
    return [jnp.exp2(sj - mx).astype(BF16) for sj in scores]


def _skewed_heads(n_heads, score, weighted_sum, probabilities=_probabilities):
    scores, probs, outs = {}, {}, []
    for i in range(n_heads + 2):
        if i < n_heads:
            scores[i] = score(i)
        if 0 <= i - 1 < n_heads:
            probs[i - 1] = probabilities(scores.pop(i - 1))
        if 0 <= i - 2 < n_heads:
            outs.append(weighted_sum(i - 2, probs.pop(i - 2)))
    return outs


def _fill_band_bias(bias_scr, gen_ref, tq, first_variant):
    variants, n_heads, _, nk = bias_scr.shape
    assert nk + tq - 1 <= BIAS_ROW
    qi = lax.broadcasted_iota(jnp.int32, (tq, nk), 0)
    km = lax.broadcasted_iota(jnp.int32, (tq, nk), 1)
    shift = CHUNK.bit_length() - 1
    dist = (qi >> shift) + BAND_CHUNKS - (km >> shift)
    in_band = (dist >= 0) & (dist <= BAND_CHUNKS)
    for h in range(n_heads):
        rows = jnp.broadcast_to(gen_ref[h], (tq, BIAS_ROW))
        toeplitz = pltpu.roll(rows, 0, 1, stride=1, stride_axis=0)[:, :nk]
        for v in range(variants):
            first_valid = BAND_PAST - (first_variant + v) * tq
            valid = in_band if first_valid <= 0 else in_band & (km >= first_valid)
            bias_scr[v, h] = jnp.where(valid, toeplitz, NEG)


def _pair_outputs(terms):
    return jnp.concatenate([terms[2 * p] + terms[2 * p + 1] for p in range(len(terms) // 2)],
                           axis=-1)


def _band_prompt_kernel(*refs, tq, nkb, sub):
    nkv = nkb - 1 + sub
    q_ref = refs[0]
    k_refs = refs[1:1 + nkv]
    v_refs = refs[1 + nkv:1 + 2 * nkv]
    gen_ref, onb_ref, yb_ref, bias_scr = refs[1 + 2 * nkv:]
    keep_bf, keep_f, ones_bf, den_lane = _pair_consts()

    @pl.when((pl.program_id(0) == 0) & (pl.program_id(1) == 0))
    def _():
        _fill_band_bias(bias_scr, gen_ref, tq, 0)

    variants = [jnp.minimum(pl.program_id(1) * sub + s, nkb - 1) for s in range(sub)]

    def score(i):
        s, h = divmod(i, N_HEADS_B)
        p, hh = divmod(h, 2)
        cols = slice(p * LANES, (p + 1) * LANES)
        qh = q_ref[0, s * tq:(s + 1) * tq, cols] * keep_bf[hh]
        return [_dot_nt(qh, k_refs[s + j][0, :, cols])
                + bias_scr[variants[s], h, :, j * tq:(j + 1) * tq] for j in range(nkb)]

    def weighted_sum(i, probs):
        s, h = divmod(i, N_HEADS_B)
        p, hh = divmod(h, 2)
        cols = slice(p * LANES, (p + 1) * LANES)
        acc = None
        for j, e in enumerate(probs):
            term = _dot(e, v_refs[s + j][0, :, cols] * keep_bf[hh] + ones_bf[hh])
            acc = term if acc is None else acc + term
        den = acc[:, den_lane[hh]:den_lane[hh] + 1]
        return acc * keep_f[hh] / den

    terms = _skewed_heads(sub * N_HEADS_B, score, weighted_sum)
    onb = onb_ref[...]
    for s in range(sub):
        o = _pair_outputs(terms[s * N_HEADS_B:(s + 1) * N_HEADS_B])
        yb_ref[0, s * tq:(s + 1) * tq, :] = _rms(o, onb).astype(BF16)


def _band_prompt(q, k, v, bias_gen, onb, layer, *, tq, sub, name):
    nb, seq, d_att = q.shape
    assert BAND_PAST % tq == 0 and seq % (sub * tq) == 0 and tq % CHUNK == 0 and tq <= MAX_TQ
    nkb = BAND_PAST // tq + 1
    nkv = nkb - 1 + sub
    grid = (nb, seq // (sub * tq))

    def kv_spec(j):
        back = nkb - 1 - j
        return pl.BlockSpec((1, tq, d_att), lambda b, t: (b, jnp.maximum(t * sub - back, 0), 0))

    tile = pl.BlockSpec((1, sub * tq, d_att), lambda b, t: (b, t, 0))
    in_specs = ([tile] + [kv_spec(j) for j in range(nkv)] + [kv_spec(j) for j in range(nkv)]
                + [_layer_spec(bias_gen, layer), _layer_spec(onb, layer)])
    return pl.pallas_call(
        functools.partial(_band_prompt_kernel, tq=tq, nkb=nkb, sub=sub),
        grid=grid, in_specs=in_specs, out_specs=tile,
        out_shape=jax.ShapeDtypeStruct((nb, seq, d_att), BF16),
        scratch_shapes=[pltpu.VMEM((nkb, N_HEADS_B, tq, nkb * tq), F32)],
        compiler_params=_params(2), name=name,
    )(q, *([k] * nkv), *([v] * nkv), bias_gen, onb)


def _band_sample_kernel(q_ref, kn_ref, vn_ref, kct_ref, vct_ref, gen_ref, onb_ref, yb_ref,
                        bias_scr, *, bb):
    ts = q_ref.shape[1]
    past = kct_ref.shape[-1]

    @pl.when(pl.program_id(0) == 0)
    def _():
        _fill_band_bias(bias_scr, gen_ref, ts, BAND_PAST // ts)

    bias_ref = bias_scr.at[0]
    keep_bf, keep_f, ones_bf, den_lane = _pair_consts()
    row = lax.broadcasted_iota(jnp.int32, (LANES, past), 0)
    keep_rows = ((row < HEAD_DIM_B).astype(BF16), (row >= HEAD_DIM_B).astype(BF16))
    ones_rows = tuple((row == dl).astype(BF16) for dl in den_lane)
    onb = onb_ref[...]

    def body(i, carry):
        kct = kct_ref[i]
        vct = vct_ref[i]

        def score(h):
            p, hh = divmod(h, 2)
            cols = slice(p * LANES, (p + 1) * LANES)
            qh = q_ref[i, :, cols] * keep_bf[hh]
            ktp = kct[2 * p:2 * p + 2].reshape(LANES, past).astype(BF16)
            return [_dot(qh, ktp) + bias_ref[h, :, :past],
                    _dot_nt(qh, kn_ref[i, :, cols]) + bias_ref[h, :, past:]]

        def weighted_sum(h, probs):
            p, hh = divmod(h, 2)
            cols = slice(p * LANES, (p + 1) * LANES)
            vtp = vct[2 * p:2 * p + 2].reshape(LANES, past).astype(BF16)
            acc = (_dot_nt(probs[0], vtp * keep_rows[hh] + ones_rows[hh])
                   + _dot(probs[1], vn_ref[i, :, cols] * keep_bf[hh] + ones_bf[hh]))
            den = acc[:, den_lane[hh]:den_lane[hh] + 1]
            return acc * keep_f[hh] / den

        o = _pair_outputs(_skewed_heads(N_HEADS_B, score, weighted_sum))
        yb_ref[i] = _rms(o, onb).astype(BF16)
        return carry

    lax.fori_loop(0, bb, body, 0)


def _band_sample(q, kn, vn, kct, vct, bias_gen, onb, layer, *, bb, name):
    nb, ts, d_att = q.shape
    past = kct.shape[-1]
    assert nb % bb == 0 and past == BAND_PAST

    def seq_spec(rows):
        return pl.BlockSpec((bb, rows, d_att), lambda b: (b, 0, 0))

    cache_in = pl.BlockSpec((None, bb, N_HEADS_B, HEAD_DIM_B, past), lambda b: (layer, b, 0, 0, 0))
    in_specs = [seq_spec(ts)] * 3 + [cache_in, cache_in, _layer_spec(bias_gen, layer),
                                     _layer_spec(onb, layer)]
    return pl.pallas_call(
        functools.partial(_band_sample_kernel, bb=bb),
        grid=(nb // bb,), in_specs=in_specs, out_specs=seq_spec(ts),
        out_shape=jax.ShapeDtypeStruct((nb, ts, d_att), BF16),
        scratch_shapes=[pltpu.VMEM((1, N_HEADS_B, ts, past + ts), F32)],
        compiler_params=_params(1), name=name,
    )(q, kn, vn, kct, vct, bias_gen, onb)


def _band_cache_update_kernel(kct_ref, vct_ref, kf_ref, vf_ref, knew_ref, vnew_ref, *, bb):
    ts = kf_ref.shape[1]
    past = kct_ref.shape[-1]
    lane = lax.broadcasted_iota(jnp.int32, (1, LANES), 1)

    def body(i, carry):
        for old_ref, new_ref, out_ref in ((kct_ref, kf_ref, knew_ref), (vct_ref, vf_ref, vnew_ref)):
            old2 = old_ref[i].reshape(N_HEADS_B * HEAD_DIM_B, past)
            shifted = pltpu.roll(old2, past - ts, axis=1)
            pad_rows = jnp.zeros((LANES - ts, new_ref.shape[-1]), F32)
            new_t = jnp.concatenate([pad_rows, new_ref[i]], axis=0).T
            last = jnp.where(lane >= LANES - ts, new_t, shifted[:, past - LANES:])
            out_ref[i, :, :, :past - LANES] = shifted[:, :past - LANES].reshape(
                N_HEADS_B, HEAD_DIM_B, past - LANES)
            out_ref[i, :, :, past - LANES:] = last.reshape(N_HEADS_B, HEAD_DIM_B, LANES)
        return carry

    lax.fori_loop(0, bb, body, 0)


def _band_cache_update(kct, vct, kf, vf, *, bb):
    depth, nb, n_heads, hd, past = kct.shape
    ts, d_att = kf.shape[2:]
    assert nb % bb == 0 and ts < LANES and past >= 2 * LANES and d_att == n_heads * hd
    cache = pl.BlockSpec((None, bb, n_heads, hd, past), lambda l, b: (l, b, 0, 0, 0))
    rows = pl.BlockSpec((None, bb, ts, d_att), lambda l, b: (l, b, 0, 0))
    shape = jax.ShapeDtypeStruct(kct.shape, F32)
    return pl.pallas_call(
        functools.partial(_band_cache_update_kernel, bb=bb),
        grid=(depth, nb // bb), in_specs=[cache, cache, rows, rows], out_specs=(cache, cache),
        out_shape=(shape, shape), compiler_params=_params(2), name="band_cache_update",
    )(kct, vct, kf, vf)


def _bias_generator(rel_bias):
    depth, n_heads, n_rel = rel_bias.shape
    assert n_rel == 2 * REL_CLIP + 1 and MAX_TQ > REL_CLIP
    rb = rel_bias.astype(F32) * LOG2E
    far = rb[:, :, n_rel - 1:]
    pieces = [jnp.broadcast_to(far, (depth, n_heads, BAND_PAST - REL_CLIP)),
              rb[:, :, ::-1],
              jnp.broadcast_to(rb[:, :, :1], (depth, n_heads, MAX_TQ - REL_CLIP - 1)),
              jnp.broadcast_to(far, (depth, n_heads, BIAS_ROW - BAND_PAST - MAX_TQ))]
    row = jnp.concatenate(pieces, axis=-1)
    assert row.shape[-1] == BIAS_ROW
    return row[:, :, None, :]


def _post_kernel(x_ref, ya_ref, yb_ref, yc_ref, mk_ref, mv_ref, wout_ref, nx_ref, wxq_ref,
                 xqn_ref, wxo_ref, nff_ref, wup_ref, wdn_ref, o_ref, *, bb, ts, ff_chunk):
    m = bb * ts
    d = x_ref.shape[-1]
    x = x_ref[...].reshape(m, d)
    y = jnp.concatenate([ya_ref[...].reshape(m, -1), yb_ref[...].reshape(m, -1),
                         yc_ref[...].reshape(m, -1)], axis=-1)
    x1 = x + _dot(y, wout_ref[...])

    qx = _dot(_rms(x1, nx_ref[...]).astype(BF16), wxq_ref[...])
    xqn = xqn_ref[...]
    qhs = [(_rms(qx[:, h * HEAD_DIM_X:(h + 1) * HEAD_DIM_X], xqn)
            * (HEAD_DIM_X ** -0.5 * LOG2E)).astype(BF16) for h in range(N_HEADS_X)]

    def score(i):
        h, b = divmod(i, bb)
        return _dot_nt(qhs[h][b * ts:(b + 1) * ts], mk_ref[b, h])

    def softmax(s):
        e = jnp.exp2(s - s.max(axis=-1, keepdims=True))
        return e.astype(BF16), e.sum(axis=-1, keepdims=True)

    def weighted_sum(i, prob_den):
        h, b = divmod(i, bb)
        return _dot(prob_den[0], mv_ref[b, h]) / prob_den[1]

    outs = _skewed_heads(N_HEADS_X * bb, score, weighted_sum, softmax)
    heads = [outs[h * bb] if bb == 1 else jnp.concatenate(outs[h * bb:(h + 1) * bb], axis=0)
             for h in range(N_HEADS_X)]
    o = jnp.concatenate(heads, axis=-1).astype(BF16)
    x2 = x1 + _dot(o, wxo_ref[...])

    xn = _rms(x2, nff_ref[...]).astype(BF16)
    d_ff = wup_ref.shape[1]
    acc = x2
    for c0 in range(0, d_ff, ff_chunk):
        hcol = jnp.maximum(_dot(xn, wup_ref[:, c0:c0 + ff_chunk]), 0.0)
        acc = acc + _dot((hcol * hcol).astype(BF16), wdn_ref[c0:c0 + ff_chunk, :])
    o_ref[...] = acc.reshape(bb, ts, d)


def _post(x, ya, yb, yc, mk, mv, layer_w, layer, *, bb, ts, name):
    nb, seq, d = x.shape
    assert nb % bb == 0 and seq % ts == 0
    grid = (nb // bb, seq // ts)

    def tok(w):
        return pl.BlockSpec((bb, ts, w), lambda b, t: (b, t, 0))

    def weight(a):
        zeros = (0,) * (a.ndim - 1)
        return pl.BlockSpec((None,) + a.shape[1:], lambda b, t: (layer,) + zeros,
                            pipeline_mode=pl.Buffered(1))

    mem = pl.BlockSpec((None, bb) + mk.shape[2:], lambda b, t: (layer, b, 0, 0, 0))
    in_specs = ([tok(d), tok(ya.shape[-1]), tok(yb.shape[-1]), tok(yc.shape[-1]), mem, mem]
                + [weight(a) for a in layer_w])
    return pl.pallas_call(
        functools.partial(_post_kernel, bb=bb, ts=ts, ff_chunk=1024),
        grid=grid, in_specs=in_specs, out_specs=tok(d),
        out_shape=jax.ShapeDtypeStruct((nb, seq, d), F32),
        compiler_params=_params(2), name=name,
    )(x, ya, yb, yc, mk, mv, *layer_w)


def _memory_kv_kernel(mem_ref, nmem_ref, wxk_ref, wxv_ref, xkn_ref, mk_ref, mv_ref, mkh_ref,
                      mvh_ref):
    mn = _rms(mem_ref[0], nmem_ref[0]).astype(BF16)
    zk = _dot(mn, wxk_ref[0])
    zv = _dot(mn, wxv_ref[0])
    xkn = xkn_ref[0]
    heads = [_rms(zk[:, h * HEAD_DIM_X:(h + 1) * HEAD_DIM_X], xkn) for h in range(N_HEADS_X)]
    mk_ref[0, 0] = jnp.concatenate(heads, axis=-1)
    mv_ref[0, 0] = zv
    for h in range(N_HEADS_X):
        mkh_ref[0, 0, h] = heads[h].astype(BF16)
        mvh_ref[0, 0, h] = zv[:, h * HEAD_DIM_X:(h + 1) * HEAD_DIM_X].astype(BF16)


def _memory_kv(mem, nmem, wxk, wxv, xkn):
    depth, d, dk = wxk.shape
    nb, n_mem, _ = mem.shape
    out = jax.ShapeDtypeStruct((depth, nb, n_mem, dk), F32)
    out_h = jax.ShapeDtypeStruct((depth, nb, N_HEADS_X, n_mem, HEAD_DIM_X), BF16)
    out_spec = pl.BlockSpec((1, 1, n_mem, dk), lambda l, b: (l, b, 0, 0))
    out_h_spec = pl.BlockSpec((1, 1, N_HEADS_X, n_mem, HEAD_DIM_X), lambda l, b: (l, b, 0, 0, 0))
    return pl.pallas_call(
        _memory_kv_kernel, grid=(depth, nb),
        in_specs=[pl.BlockSpec((1, n_mem, d), lambda l, b: (b, 0, 0)),
                  pl.BlockSpec((1, 1, d), lambda l, b: (l, 0, 0)),
                  pl.BlockSpec((1, d, dk), lambda l, b: (l, 0, 0)),
                  pl.BlockSpec((1, d, dk), lambda l, b: (l, 0, 0)),
                  pl.BlockSpec((1, 1, HEAD_DIM_X), lambda l, b: (l, 0, 0))],
        out_specs=(out_spec, out_spec, out_h_spec, out_h_spec),
        out_shape=(out, out, out_h, out_h),
        compiler_params=_params(2), name="memory_kv",
    )(mem, nmem, wxk, wxv, xkn)


def _tile_config(nb, seq):
    ts = min(BAND_PAST, seq)
    bb = max(1, min(nb, BAND_PAST // ts))
    while nb % bb:
        bb -= 1
    return bb, ts


def kernel(x_prompt, x_sample, cache_conv_a, cache_band_k, cache_band_v, cache_conv_c, cache_mem_k, cache_mem_v, mem_prompt, norm_mix, w_in, conv_a_w, conv_a_b, ln_a_g, ln_a_b, q_norm, k_norm, rel_bias, conv_c_w, out_norm, w_out, norm_x, norm_mem, w_xq, w_xk, w_xv, xq_norm, xk_norm, w_xo, norm_ff, w_up, w_down):
    depth = w_in.shape[0]
    bp, seq_p, d = x_prompt.shape
    bs, seq_s, _ = x_sample.shape
    d_conv = conv_a_w.shape[-1]
    d_sc = conv_c_w.shape[-1]
    d_att = N_HEADS_B * HEAD_DIM_B
    past = cache_band_k.shape[2]
    assert past == BAND_PAST and seq_s == CHUNK, "sample step: one chunk behind a full band"

    bb_p, ts_p = _tile_config(bp, seq_p)
    bb_s, ts_s = _tile_config(bs, seq_s)
    ts_mix_p = 2 * ts_p if bb_p == 1 and seq_p % (2 * ts_p) == 0 else ts_p
    tq_p = min(MAX_TQ, seq_p)
    sub_p = max(s for s in (1, 2, 4) if seq_p % (s * tq_p) == 0)
    bb_band_s = 4 if bs % 4 == 0 else 1

    row = lambda a: a[:, None, :]
    w_in_b, w_out_b, w_xq_b, w_xk_b, w_xv_b, w_xo_b, w_up_b, w_down_b = (
        a.astype(BF16) for a in (w_in, w_out, w_xq, w_xk, w_xv, w_xo, w_up, w_down))
    nmem, xkn = row(norm_mem), row(xk_norm)
    onb = row(out_norm[:, d_conv:d_conv + d_att])
    mix_w = (row(norm_mix), w_in_b, conv_a_w, row(conv_a_b), row(ln_a_g), row(ln_a_b),
             row(jnp.tile(q_norm, (1, N_HEADS_B))), row(jnp.tile(k_norm, (1, N_HEADS_B))),
             conv_c_w, row(out_norm[:, :d_conv]), row(out_norm[:, d_conv + d_att:]))
    post_w = (w_out_b, row(norm_x), w_xq_b, row(xq_norm), w_xo_b, row(norm_ff), w_up_b, w_down_b)
    head_id = jnp.arange(d_att) // HEAD_DIM_B
    hsum = (head_id[:, None] == head_id[None, :]).astype(BF16)

    bias_gen = _bias_generator(rel_bias)

    mk_p, mv_p, mk_ph, mv_ph = _memory_kv(mem_prompt, nmem, w_xk_b, w_xv_b, xkn)
    mk_sh = jnp.transpose(cache_mem_k, (0, 1, 3, 2, 4)).astype(BF16)
    mv_sh = jnp.transpose(cache_mem_v, (0, 1, 3, 2, 4)).astype(BF16)
    kct = jnp.transpose(cache_band_k, (0, 1, 3, 4, 2))
    vct = jnp.transpose(cache_band_v, (0, 1, 3, 4, 2))

    zero_a = jnp.zeros((bp, CONV_A_WIDTH - 1, d_conv), F32)
    zero_c = jnp.zeros((bp, CONV_C_WIDTH - 1, d_sc), F32)

    yp, ys = x_prompt, x_sample
    ca_p, kb_p, vb_p, cc_p = [], [], [], []
    ca_s, kb_s, vb_s, cc_s = [], [], [], []
    for l in range(depth):
        ya, yc, q, k, v, sa, sc, kf, vf = _mixer_in(
            yp, zero_a, zero_c, None, mix_w, hsum, l, bb=bb_p, ts=ts_mix_p, name=f"mixer_in_p{l}")
        yb = _band_prompt(q, k, v, bias_gen, onb, l, tq=tq_p, sub=sub_p, name=f"band_p{l}")
        yp = _post(yp, ya, yb, yc, mk_ph, mv_ph, post_w, l, bb=bb_p, ts=ts_p, name=f"post_p{l}")
        ca_p.append(sa); cc_p.append(sc); kb_p.append(kf); vb_p.append(vf)

        ya, yc, q, k, v, sa, sc, kf, vf = _mixer_in(
            ys, cache_conv_a, cache_conv_c, l, mix_w, hsum, l, bb=bb_s, ts=ts_s,
            name=f"mixer_in_s{l}")
        yb = _band_sample(q, k, v, kct, vct, bias_gen, onb, l, bb=bb_band_s, name=f"band_s{l}")
        ys = _post(ys, ya, yb, yc, mk_sh, mv_sh, post_w, l, bb=bb_s, ts=ts_s, name=f"post_s{l}")
        ca_s.append(sa); cc_s.append(sc); kb_s.append(kf); vb_s.append(vf)

    knew, vnew = _band_cache_update(kct, vct, jnp.stack(kb_s), jnp.stack(vb_s), bb=bb_band_s)

    def rows_major(cache_t):
        return jnp.transpose(cache_t, (0, 1, 4, 2, 3))

    def split_heads(parts, nb):
        return jnp.stack(parts).reshape(depth, nb, -1, N_HEADS_B, HEAD_DIM_B)

    return (yp, ys,
            jnp.stack(ca_p), split_heads(kb_p, bp), split_heads(vb_p, bp), jnp.stack(cc_p),
            mk_p.reshape(depth, bp, -1, N_HEADS_X, HEAD_DIM_X),
            mv_p.reshape(depth, bp, -1, N_HEADS_X, HEAD_DIM_X),
            jnp.stack(ca_s), rows_major(knew), rows_major(vnew), jnp.stack(cc_s))
```

```python
import functools
import math

import jax
import jax.numpy as jnp
from jax import lax
from jax.experimental import pallas as pl
from jax.experimental.pallas import tpu as pltpu

EPS = 1e-6
NEG = -1e30
LOG2E = math.log2(math.e)
CHUNK = 64
BAND_CHUNKS = 8
BAND_PAST = BAND_CHUNKS * CHUNK
REL_CLIP = 128
N_HEADS_B = 8
HEAD_DIM_B = 64
N_HEADS_X = 4
HEAD_DIM_X = 256
CONV_A_WIDTH = 31
CONV_C_WIDTH = 3

SUBLANES = 8
LANES = 128
PAD_A = 32
PAD_C = 8
CONV_ROWS = 64
MIX_GROUP_ROWS = 512
MAX_TQ = 256
BIAS_ROW = 1024
VMEM_LIMIT = 56 * 1024 * 1024

F32 = jnp.float32
BF16 = jnp.bfloat16


def _rms(x, g):
    return x * lax.rsqrt(jnp.mean(x * x, axis=-1, keepdims=True) + EPS) * g


def _ordering_zero(x):
    bits = lax.bitcast_convert_type(x, jnp.uint32)
    return lax.bitcast_convert_type((bits >> 16) >> 16, F32)


def _dot(a, b):
    return jnp.dot(a, b, preferred_element_type=F32)


def _dot_nt(a, b):
    return lax.dot_general(a, b, (((1,), (1,)), ((), ())), preferred_element_type=F32)


def _layer_spec(a, layer, single_buffer=False):
    zeros = (0,) * (a.ndim - 1)
    mode = dict(pipeline_mode=pl.Buffered(1)) if single_buffer else {}
    return pl.BlockSpec((None,) + a.shape[1:], lambda *_: (layer,) + zeros, **mode)


def _params(n_axes):
    return pltpu.CompilerParams(dimension_semantics=("arbitrary",) * n_axes,
                                vmem_limit_bytes=VMEM_LIMIT)


def _mixer_in_kernel(x_ref, ctxa_ref, ctxc_ref, nmix_ref, win_ref, caw_ref, cab_ref, lng_ref,
                     lnb_ref, qn_ref, kn_ref, ccw_ref, ona_ref, onc_ref, hsum_ref,
                     ya_ref, yc_ref, q_ref, k_ref, v_ref, newa_ref, newc_ref, kf_ref, vf_ref,
                     ua_scr, uc_scr, *, bb, ts, d_conv, d_att, d_sc):
    t = pl.program_id(1)
    m = bb * ts
    ctx_a = CONV_A_WIDTH - 1
    ctx_c = CONV_C_WIDTH - 1

    @pl.when(t == 0)
    def _():
        ua_scr[:, PAD_A - ctx_a:PAD_A, :] = ctxa_ref[...]
        uc_scr[:, PAD_C - ctx_c:PAD_C, :] = ctxc_ref[...]

    @pl.when(t > 0)
    def _():
        ua_scr[:, PAD_A - ctx_a:PAD_A, :] = ua_scr[:, PAD_A + ts - ctx_a:PAD_A + ts, :]
        uc_scr[:, PAD_C - ctx_c:PAD_C, :] = uc_scr[:, PAD_C + ts - ctx_c:PAD_C + ts, :]

    x = x_ref[...].reshape(m, x_ref.shape[-1])
    xn = _rms(x, nmix_ref[...]).astype(BF16)

    c0 = 0
    c1 = 2 * d_conv
    c2 = c1 + 3 * d_att
    c3 = c2 + 3 * d_sc

    n_groups = m // MIX_GROUP_ROWS if bb == 1 and m % MIX_GROUP_ROWS == 0 else 1
    rows_g = m // n_groups
    held = {}

    def xn_rows(g):
        return xn[g * rows_g:(g + 1) * rows_g]

    def put_rows(ref, g, val, pad=0):
        if n_groups == 1:
            ref[:, pad:pad + ts, :] = val.reshape(bb, ts, val.shape[-1])
        else:
            ref[0, pad + g * rows_g:pad + (g + 1) * rows_g, :] = val

    def store_tail(ref, g, val):
        first = ts - ref.shape[1]
        if n_groups == 1:
            ref[...] = val.reshape(bb, ts, d_att)[:, first:, :]
        else:
            lo, hi = max(g * rows_g, first), (g + 1) * rows_g
            if lo < hi:
                ref[0, lo - first:hi - first, :] = val[lo - g * rows_g:hi - g * rows_g]

    def head_rms(z, gain):
        ss = _dot((z * z).astype(BF16), hsum_ref[...]) * (1.0 / HEAD_DIM_B)
        return z * lax.rsqrt(ss + EPS) * gain

    def token(z):
        return z[0:1, :d_conv]

    def stage_a_conv_input(g):
        za = _dot(xn_rows(g), win_ref[:, c0:c1])
        put_rows(ua_scr, g, za[:, :d_conv] * jax.nn.sigmoid(za[:, d_conv:]), PAD_A)
        if g == n_groups - 1:
            newa_ref[...] = ua_scr[:, PAD_A + ts - ctx_a:PAD_A + ts, :]
        return token(za)

    def stage_c_conv_input(g):
        z = _dot(xn_rows(g), win_ref[:, c2 + d_sc:c3])
        put_rows(uc_scr, g, z[:, :d_sc] * z[:, d_sc:], PAD_C)
        if g == n_groups - 1:
            newc_ref[...] = uc_scr[:, PAD_C + ts - ctx_c:PAD_C + ts, :]
        return token(z)

    def stage_c_gate(g):
        held["g_b", g] = _dot(xn_rows(g), win_ref[:, c2:c2 + d_sc])
        return token(held["g_b", g])

    def stage_q_proj(g):
        held["zq", g] = _dot(xn_rows(g), win_ref[:, c1:c1 + d_att])
        return token(held["zq", g])

    def stage_q_norm(g):
        qh = head_rms(held.pop(("zq", g)), qn_ref[...]) * (HEAD_DIM_B ** -0.5 * LOG2E)
        put_rows(q_ref, g, qh.astype(BF16))
        return token(qh)

    def stage_k_proj(g):
        held["zk", g] = _dot(xn_rows(g), win_ref[:, c1 + d_att:c1 + 2 * d_att])
        return token(held["zk", g])

    def stage_k_norm(g):
        kh = head_rms(held.pop(("zk", g)), kn_ref[...])
        put_rows(k_ref, g, kh.astype(BF16))
        store_tail(kf_ref, g, kh)
        return token(kh)

    def stage_v_proj(g):
        zv = _dot(xn_rows(g), win_ref[:, c1 + 2 * d_att:c2])
        put_rows(v_ref, g, zv.astype(BF16))
        store_tail(vf_ref, g, zv)
        return token(zv)

    per_group = (stage_c_conv_input, stage_c_gate, stage_q_proj, stage_q_norm, stage_k_proj,
                 stage_k_norm, stage_v_proj)
    stage_a_conv_input(0)
    stages = ([(stage_a_conv_input, g) for g in range(1, n_groups)]
              + [(fn, g) for g in range(n_groups) for fn in per_group])
    done = set()

    cab = cab_ref[...]
    lng = lng_ref[...]
    lnb = lnb_ref[...]
    ona = ona_ref[...]
    onc = onc_ref[...]
    taps_a = [caw_ref[j:j + 1, :] for j in range(CONV_A_WIDTH)]
    taps_c = [ccw_ref[j:j + 1, :] for j in range(CONV_C_WIDTH)]

    def conformer_conv_block(b, r0, after):
        taps = taps_a if after is None else [tap + _ordering_zero(after) for tap in taps_a]
        base = PAD_A - ctx_a + r0
        parts = []
        for res in range(SUBLANES):
            js = [j for j in range(CONV_A_WIDTH) if (base + j) % SUBLANES == res]
            if not js:
                continue
            first = base + js[0]
            win = ua_scr[b, first:first + (js[-1] - js[0]) + CONV_ROWS, :]
            part = None
            for j in js:
                off = j - js[0]
                term = win[off:off + CONV_ROWS, :] * taps[j]
                part = term if part is None else part + term
            parts.append(part)
        conv = parts[0]
        for part in parts[1:]:
            conv = conv + part
        conv = conv + cab
        mu = jnp.mean(conv, axis=-1, keepdims=True)
        cen = conv - mu
        var = jnp.mean(cen * cen, axis=-1, keepdims=True)
        y = cen * lax.rsqrt(var + EPS) * lng + lnb
        y = y * jax.nn.sigmoid(y)
        ya_ref[b, r0:r0 + CONV_ROWS, :] = _rms(y, ona).astype(BF16)

    def short_conv_block(b, r0):
        basec = PAD_C - ctx_c + r0
        convc = None
        for j in range(CONV_C_WIDTH):
            term = uc_scr[b, basec + j:basec + j + CONV_ROWS, :] * taps_c[j]
            convc = term if convc is None else convc + term
        g, row0 = divmod(b * ts + r0, rows_g)
        yc = held["g_b", g][row0:row0 + CONV_ROWS, :] * convc
        yc_ref[b, r0:r0 + CONV_ROWS, :] = _rms(yc, onc).astype(BF16)

    def short_conv_ready(b, r0):
        g = (b * ts + r0) // rows_g
        return (stage_c_conv_input, g) in done and (stage_c_gate, g) in done

    blocks = [(b, r0) for b in range(bb) for r0 in range(0, ts, CONV_ROWS)]
    assert rows_g % CONV_ROWS == 0 and rows_g // CONV_ROWS >= n_groups
    pending_c = list(blocks)
    after = None
    for n, blk in enumerate(blocks):
        conformer_conv_block(*blk, after)
        if n < len(stages):
            fn, g = stages[n]
            after = fn(g)
            done.add(stages[n])
        if short_conv_ready(*pending_c[0]):
            short_conv_block(*pending_c.pop(0))
    for fn, g in stages[len(blocks):]:
        fn(g)
    for blk in pending_c:
        short_conv_block(*blk)


def _mixer_in(x, ctx_a, ctx_c, ctx_layer, layer_w, hsum, layer, *, bb, ts, name):
    nb, seq, d = x.shape
    d_conv = layer_w[2].shape[-1]
    d_sc = layer_w[8].shape[-1]
    d_att = hsum.shape[0]
    assert nb % bb == 0 and seq % ts == 0 and ts % CONV_ROWS == 0 and ts >= CONV_A_WIDTH - 1
    tail = min(BAND_PAST, seq)
    assert tail <= ts and tail % SUBLANES == 0, "the last tile must hold all cached band rows"
    grid = (nb // bb, seq // ts)

    def tok(w):
        return pl.BlockSpec((bb, ts, w), lambda b, t: (b, t, 0))

    def per_seq(rows, w):
        return pl.BlockSpec((bb, rows, w), lambda b, t: (b, 0, 0))

    def ctx_spec(a):
        if ctx_layer is None:
            return per_seq(a.shape[-2], a.shape[-1])
        return pl.BlockSpec((None, bb) + a.shape[2:], lambda b, t: (ctx_layer, b, 0, 0))

    kern = functools.partial(_mixer_in_kernel, bb=bb, ts=ts, d_conv=d_conv, d_att=d_att, d_sc=d_sc)
    tail_shape = (nb, tail, d_att)
    tail_spec = per_seq(tail, d_att)
    out_shape = (
        jax.ShapeDtypeStruct((nb, seq, d_conv), BF16),
        jax.ShapeDtypeStruct((nb, seq, d_sc), BF16),
        jax.ShapeDtypeStruct((nb, seq, d_att), BF16),
        jax.ShapeDtypeStruct((nb, seq, d_att), BF16),
        jax.ShapeDtypeStruct((nb, seq, d_att), BF16),
        jax.ShapeDtypeStruct((nb, CONV_A_WIDTH - 1, d_conv), F32),
        jax.ShapeDtypeStruct((nb, CONV_C_WIDTH - 1, d_sc), F32),
        jax.ShapeDtypeStruct(tail_shape, F32),
        jax.ShapeDtypeStruct(tail_shape, F32),
    )
    out_specs = (
        tok(d_conv), tok(d_sc), tok(d_att), tok(d_att), tok(d_att),
        per_seq(CONV_A_WIDTH - 1, d_conv), per_seq(CONV_C_WIDTH - 1, d_sc), tail_spec, tail_spec,
    )
    in_specs = ([tok(d), ctx_spec(ctx_a), ctx_spec(ctx_c)]
                + [_layer_spec(a, layer, single_buffer=True) for a in layer_w]
                + [pl.BlockSpec(hsum.shape, lambda b, t: (0, 0), pipeline_mode=pl.Buffered(1))])
    return pl.pallas_call(
        kern, grid=grid, in_specs=in_specs, out_specs=out_specs, out_shape=out_shape,
        scratch_shapes=[pltpu.VMEM((bb, PAD_A + ts, d_conv), F32),
                        pltpu.VMEM((bb, PAD_C + ts, d_sc), F32)],
        compiler_params=_params(2), name=name,
    )(x, ctx_a, ctx_c, *layer_w, hsum)


def _pair_consts():
    lane = lax.broadcasted_iota(jnp.int32, (1, LANES), 1)
    lo = lane < HEAD_DIM_B
    hi = lane >= HEAD_DIM_B
    keep_bf = (lo.astype(BF16), hi.astype(BF16))
    keep_f = (lo.astype(F32), hi.astype(F32))
    den_lane = (HEAD_DIM_B, 0)
    ones_bf = tuple((lane == dl).astype(BF16) for dl in den_lane)
    return keep_bf, keep_f, ones_bf, den_lane


def _probabilities(scores):
    mx = scores[0].max(axis=-1, keepdims=True)
    for sj in scores[1:]:
        mx = jnp.maximum(mx, sj.max(axis=-1, keepdims=True))
    return [jnp.exp2(sj - mx).astype(BF16) for sj in scores]


def _skewed_heads(n_heads, score, weighted_sum, probabilities=_probabilities):
    scores, probs, outs = {}, {}, []
    for i in range(n_heads + 2):
        if i < n_heads:
            scores[i] = score(i)
        if 0 <= i - 1 < n_heads:
            probs[i - 1] = probabilities(scores.pop(i - 1))
        if 0 <= i - 2 < n_heads:
            outs.append(weighted_sum(i - 2, probs.pop(i - 2)))
    return outs


def _fill_band_bias(bias_scr, gen_ref, tq, first_variant):
    variants, n_heads, _, nk = bias_scr.shape
    assert nk + tq - 1 <= BIAS_ROW
    qi = lax.broadcasted_iota(jnp.int32, (tq, nk), 0)
    km = lax.broadcasted_iota(jnp.int32, (tq, nk), 1)
    shift = CHUNK.bit_length() - 1
    dist = (qi >> shift) + BAND_CHUNKS - (km >> shift)
    in_band = (dist >= 0) & (dist <= BAND_CHUNKS)
    for h in range(n_heads):
        rows = jnp.broadcast_to(gen_ref[h], (tq, BIAS_ROW))
        toeplitz = pltpu.roll(rows, 0, 1, stride=1, stride_axis=0)[:, :nk]
        for v in range(variants):
            first_valid = BAND_PAST - (first_variant + v) * tq
            valid = in_band if first_valid <= 0 else in_band & (km >= first_valid)
            bias_scr[v, h] = jnp.where(valid, toeplitz, NEG)


def _pair_outputs(terms):
    return jnp.concatenate([terms[2 * p] + terms[2 * p + 1] for p in range(len(terms) // 2)],
                           axis=-1)


def _band_prompt_kernel(*refs, tq, nkb, sub):
    nkv = nkb - 1 + sub
    q_ref = refs[0]
    k_refs = refs[1:1 + nkv]
    v_refs = refs[1 + nkv:1 + 2 * nkv]
    gen_ref, onb_ref, yb_ref, bias_scr = refs[1 + 2 * nkv:]
    keep_bf, keep_f, ones_bf, den_lane = _pair_consts()

    @pl.when((pl.program_id(0) == 0) & (pl.program_id(1) == 0))
    def _():
        _fill_band_bias(bias_scr, gen_ref, tq, 0)

    variants = [jnp.minimum(pl.program_id(1) * sub + s, nkb - 1) for s in range(sub)]

    def score(i):
        s, h = divmod(i, N_HEADS_B)
        p, hh = divmod(h, 2)
        cols = slice(p * LANES, (p + 1) * LANES)
        qh = q_ref[0, s * tq:(s + 1) * tq, cols] * keep_bf[hh]
        return [_dot_nt(qh, k_refs[s + j][0, :, cols])
                + bias_scr[variants[s], h, :, j * tq:(j + 1) * tq] for j in range(nkb)]

    def weighted_sum(i, probs):
        s, h = divmod(i, N_HEADS_B)
        p, hh = divmod(h, 2)
        cols = slice(p * LANES, (p + 1) * LANES)
        acc = None
        for j, e in enumerate(probs):
            term = _dot(e, v_refs[s + j][0, :, cols] * keep_bf[hh] + ones_bf[hh])
            acc = term if acc is None else acc + term
        den = acc[:, den_lane[hh]:den_lane[hh] + 1]
        return acc * keep_f[hh] / den

    terms = _skewed_heads(sub * N_HEADS_B, score, weighted_sum)
    onb = onb_ref[...]
    for s in range(sub):
        o = _pair_outputs(terms[s * N_HEADS_B:(s + 1) * N_HEADS_B])
        yb_ref[0, s * tq:(s + 1) * tq, :] = _rms(o, onb).astype(BF16)


def _band_prompt(q, k, v, bias_gen, onb, layer, *, tq, sub, name):
    nb, seq, d_att = q.shape
    assert BAND_PAST % tq == 0 and seq % (sub * tq) == 0 and tq % CHUNK == 0 and tq <= MAX_TQ
    nkb = BAND_PAST // tq + 1
    nkv = nkb - 1 + sub
    grid = (nb, seq // (sub * tq))

    def kv_spec(j):
        back = nkb - 1 - j
        return pl.BlockSpec((1, tq, d_att), lambda b, t: (b, jnp.maximum(t * sub - back, 0), 0))

    tile = pl.BlockSpec((1, sub * tq, d_att), lambda b, t: (b, t, 0))
    in_specs = ([tile] + [kv_spec(j) for j in range(nkv)] + [kv_spec(j) for j in range(nkv)]
                + [_layer_spec(bias_gen, layer), _layer_spec(onb, layer)])
    return pl.pallas_call(
        functools.partial(_band_prompt_kernel, tq=tq, nkb=nkb, sub=sub),
        grid=grid, in_specs=in_specs, out_specs=tile,
        out_shape=jax.ShapeDtypeStruct((nb, seq, d_att), BF16),
        scratch_shapes=[pltpu.VMEM((nkb, N_HEADS_B, tq, nkb * tq), F32)],
        compiler_params=_params(2), name=name,
    )(q, *([k] * nkv), *([v] * nkv), bias_gen, onb)


def _band_sample_kernel(q_ref, kn_ref, vn_ref, kct_ref, vct_ref, gen_ref, onb_ref, yb_ref,
                        bias_scr, *, bb):
    ts = q_ref.shape[1]
    past = kct_ref.shape[-1]

    @pl.when(pl.program_id(0) == 0)
    def _():
        _fill_band_bias(bias_scr, gen_ref, ts, BAND_PAST // ts)

    bias_ref = bias_scr.at[0]
    keep_bf, keep_f, ones_bf, den_lane = _pair_consts()
    row = lax.broadcasted_iota(jnp.int32, (LANES, past), 0)
    keep_rows = ((row < HEAD_DIM_B).astype(BF16), (row >= HEAD_DIM_B).astype(BF16))
    ones_rows = tuple((row == dl).astype(BF16) for dl in den_lane)
    onb = onb_ref[...]

    def body(i, carry):
        kct = kct_ref[i]
        vct = vct_ref[i]

        def score(h):
            p, hh = divmod(h, 2)
            cols = slice(p * LANES, (p + 1) * LANES)
            qh = q_ref[i, :, cols] * keep_bf[hh]
            ktp = kct[2 * p:2 * p + 2].reshape(LANES, past).astype(BF16)
            return [_dot(qh, ktp) + bias_ref[h, :, :past],
                    _dot_nt(qh, kn_ref[i, :, cols]) + bias_ref[h, :, past:]]

        def weighted_sum(h, probs):
            p, hh = divmod(h, 2)
            cols = slice(p * LANES, (p + 1) * LANES)
            vtp = vct[2 * p:2 * p + 2].reshape(LANES, past).astype(BF16)
            acc = (_dot_nt(probs[0], vtp * keep_rows[hh] + ones_rows[hh])
                   + _dot(probs[1], vn_ref[i, :, cols] * keep_bf[hh] + ones_bf[hh]))
            den = acc[:, den_lane[hh]:den_lane[hh] + 1]
            return acc * keep_f[hh] / den

        o = _pair_outputs(_skewed_heads(N_HEADS_B, score, weighted_sum))
        yb_ref[i] = _rms(o, onb).astype(BF16)
        return carry

    lax.fori_loop(0, bb, body, 0)


def _band_sample(q, kn, vn, kct, vct, bias_gen, onb, layer, *, bb, name):
    nb, ts, d_att = q.shape
    past = kct.shape[-1]
    assert nb % bb == 0 and past == BAND_PAST

    def seq_spec(rows):
        return pl.BlockSpec((bb, rows, d_att), lambda b: (b, 0, 0))

    cache_in = pl.BlockSpec((None, bb, N_HEADS_B, HEAD_DIM_B, past), lambda b: (layer, b, 0, 0, 0))
    in_specs = [seq_spec(ts)] * 3 + [cache_in, cache_in, _layer_spec(bias_gen, layer),
                                     _layer_spec(onb, layer)]
    return pl.pallas_call(
        functools.partial(_band_sample_kernel, bb=bb),
        grid=(nb // bb,), in_specs=in_specs, out_specs=seq_spec(ts),
        out_shape=jax.ShapeDtypeStruct((nb, ts, d_att), BF16),
        scratch_shapes=[pltpu.VMEM((1, N_HEADS_B, ts, past + ts), F32)],
        compiler_params=_params(1), name=name,
    )(q, kn, vn, kct, vct, bias_gen, onb)


def _band_cache_update_kernel(kct_ref, vct_ref, kf_ref, vf_ref, knew_ref, vnew_ref, *, bb):
    ts = kf_ref.shape[1]
    past = kct_ref.shape[-1]
    lane = lax.broadcasted_iota(jnp.int32, (1, LANES), 1)

    def body(i, carry):
        for old_ref, new_ref, out_ref in ((kct_ref, kf_ref, knew_ref), (vct_ref, vf_ref, vnew_ref)):
            old2 = old_ref[i].reshape(N_HEADS_B * HEAD_DIM_B, past)
            shifted = pltpu.roll(old2, past - ts, axis=1)
            pad_rows = jnp.zeros((LANES - ts, new_ref.shape[-1]), F32)
            new_t = jnp.concatenate([pad_rows, new_ref[i]], axis=0).T
            last = jnp.where(lane >= LANES - ts, new_t, shifted[:, past - LANES:])
            out_ref[i, :, :, :past - LANES] = shifted[:, :past - LANES].reshape(
                N_HEADS_B, HEAD_DIM_B, past - LANES)
            out_ref[i, :, :, past - LANES:] = last.reshape(N_HEADS_B, HEAD_DIM_B, LANES)
        return carry

    lax.fori_loop(0, bb, body, 0)


def _band_cache_update(kct, vct, kf, vf, *, bb):
    depth, nb, n_heads, hd, past = kct.shape
    ts, d_att = kf.shape[2:]
    assert nb % bb == 0 and ts < LANES and past >= 2 * LANES and d_att == n_heads * hd
    cache = pl.BlockSpec((None, bb, n_heads, hd, past), lambda l, b: (l, b, 0, 0, 0))
    rows = pl.BlockSpec((None, bb, ts, d_att), lambda l, b: (l, b, 0, 0))
    shape = jax.ShapeDtypeStruct(kct.shape, F32)
    return pl.pallas_call(
        functools.partial(_band_cache_update_kernel, bb=bb),
        grid=(depth, nb // bb), in_specs=[cache, cache, rows, rows], out_specs=(cache, cache),
        out_shape=(shape, shape), compiler_params=_params(2), name="band_cache_update",
    )(kct, vct, kf, vf)


def _bias_generator(rel_bias):
    depth, n_heads, n_rel = rel_bias.shape
    assert n_rel == 2 * REL_CLIP + 1 and MAX_TQ > REL_CLIP
    rb = rel_bias.astype(F32) * LOG2E
    far = rb[:, :, n_rel - 1:]
    pieces = [jnp.broadcast_to(far, (depth, n_heads, BAND_PAST - REL_CLIP)),
              rb[:, :, ::-1],
              jnp.broadcast_to(rb[:, :, :1], (depth, n_heads, MAX_TQ - REL_CLIP - 1)),
              jnp.broadcast_to(far, (depth, n_heads, BIAS_ROW - BAND_PAST - MAX_TQ))]
    row = jnp.concatenate(pieces, axis=-1)
    assert row.shape[-1] == BIAS_ROW
    return row[:, :, None, :]


def _post_kernel(x_ref, ya_ref, yb_ref, yc_ref, mk_ref, mv_ref, wout_ref, nx_ref, wxq_ref,
                 xqn_ref, wxo_ref, nff_ref, wup_ref, wdn_ref, o_ref, *, bb, ts, ff_chunk):
    m = bb * ts
    d = x_ref.shape[-1]
    x = x_ref[...].reshape(m, d)
    y = jnp.concatenate([ya_ref[...].reshape(m, -1), yb_ref[...].reshape(m, -1),
                         yc_ref[...].reshape(m, -1)], axis=-1)
    x1 = x + _dot(y, wout_ref[...])

    qx = _dot(_rms(x1, nx_ref[...]).astype(BF16), wxq_ref[...])
    xqn = xqn_ref[...]
    qhs = [(_rms(qx[:, h * HEAD_DIM_X:(h + 1) * HEAD_DIM_X], xqn)
            * (HEAD_DIM_X ** -0.5 * LOG2E)).astype(BF16) for h in range(N_HEADS_X)]

    def score(i):
        h, b = divmod(i, bb)
        return _dot_nt(qhs[h][b * ts:(b + 1) * ts], mk_ref[b, h])

    def softmax(s):
        e = jnp.exp2(s - s.max(axis=-1, keepdims=True))
        return e.astype(BF16), e.sum(axis=-1, keepdims=True)

    def weighted_sum(i, prob_den):
        h, b = divmod(i, bb)
        return _dot(prob_den[0], mv_ref[b, h]) / prob_den[1]

    outs = _skewed_heads(N_HEADS_X * bb, score, weighted_sum, softmax)
    heads = [outs[h * bb] if bb == 1 else jnp.concatenate(outs[h * bb:(h + 1) * bb], axis=0)
             for h in range(N_HEADS_X)]
    o = jnp.concatenate(heads, axis=-1).astype(BF16)
    x2 = x1 + _dot(o, wxo_ref[...])

    xn = _rms(x2, nff_ref[...]).astype(BF16)
    d_ff = wup_ref.shape[1]
    acc = x2
    for c0 in range(0, d_ff, ff_chunk):
        hcol = jnp.maximum(_dot(xn, wup_ref[:, c0:c0 + ff_chunk]), 0.0)
        acc = acc + _dot((hcol * hcol).astype(BF16), wdn_ref[c0:c0 + ff_chunk, :])
    o_ref[...] = acc.reshape(bb, ts, d)


def _post(x, ya, yb, yc, mk, mv, layer_w, layer, *, bb, ts, name):
    nb, seq, d = x.shape
    assert nb % bb == 0 and seq % ts == 0
    grid = (nb // bb, seq // ts)

    def tok(w):
        return pl.BlockSpec((bb, ts, w), lambda b, t: (b, t, 0))

    def weight(a):
        zeros = (0,) * (a.ndim - 1)
        return pl.BlockSpec((None,) + a.shape[1:], lambda b, t: (layer,) + zeros,
                            pipeline_mode=pl.Buffered(1))

    mem = pl.BlockSpec((None, bb) + mk.shape[2:], lambda b, t: (layer, b, 0, 0, 0))
    in_specs = ([tok(d), tok(ya.shape[-1]), tok(yb.shape[-1]), tok(yc.shape[-1]), mem, mem]
                + [weight(a) for a in layer_w])
    return pl.pallas_call(
        functools.partial(_post_kernel, bb=bb, ts=ts, ff_chunk=1024),
        grid=grid, in_specs=in_specs, out_specs=tok(d),
        out_shape=jax.ShapeDtypeStruct((nb, seq, d), F32),
        compiler_params=_params(2), name=name,
    )(x, ya, yb, yc, mk, mv, *layer_w)


def _memory_kv_kernel(mem_ref, nmem_ref, wxk_ref, wxv_ref, xkn_ref, mk_ref, mv_ref, mkh_ref,
                      mvh_ref):
    mn = _rms(mem_ref[0], nmem_ref[0]).astype(BF16)
    zk = _dot(mn, wxk_ref[0])
    zv = _dot(mn, wxv_ref[0])
    xkn = xkn_ref[0]
    heads = [_rms(zk[:, h * HEAD_DIM_X:(h + 1) * HEAD_DIM_X], xkn) for h in range(N_HEADS_X)]
    mk_ref[0, 0] = jnp.concatenate(heads, axis=-1)
    mv_ref[0, 0] = zv
    for h in range(N_HEADS_X):
        mkh_ref[0, 0, h] = heads[h].astype(BF16)
        mvh_ref[0, 0, h] = zv[:, h * HEAD_DIM_X:(h + 1) * HEAD_DIM_X].astype(BF16)


def _memory_kv(mem, nmem, wxk, wxv, xkn):
    depth, d, dk = wxk.shape
    nb, n_mem, _ = mem.shape
    out = jax.ShapeDtypeStruct((depth, nb, n_mem, dk), F32)
    out_h = jax.ShapeDtypeStruct((depth, nb, N_HEADS_X, n_mem, HEAD_DIM_X), BF16)
    out_spec = pl.BlockSpec((1, 1, n_mem, dk), lambda l, b: (l, b, 0, 0))
    out_h_spec = pl.BlockSpec((1, 1, N_HEADS_X, n_mem, HEAD_DIM_X), lambda l, b: (l, b, 0, 0, 0))
    return pl.pallas_call(
        _memory_kv_kernel, grid=(depth, nb),
        in_specs=[pl.BlockSpec((1, n_mem, d), lambda l, b: (b, 0, 0)),
                  pl.BlockSpec((1, 1, d), lambda l, b: (l, 0, 0)),
                  pl.BlockSpec((1, d, dk), lambda l, b: (l, 0, 0)),
                  pl.BlockSpec((1, d, dk), lambda l, b: (l, 0, 0)),
                  pl.BlockSpec((1, 1, HEAD_DIM_X), lambda l, b: (l, 0, 0))],
        out_specs=(out_spec, out_spec, out_h_spec, out_h_spec),
        out_shape=(out, out, out_h, out_h),
        compiler_params=_params(2), name="memory_kv",
    )(mem, nmem, wxk, wxv, xkn)


def _tile_config(nb, seq):
    ts = min(BAND_PAST, seq)
    bb = max(1, min(nb, BAND_PAST // ts))
    while nb % bb:
        bb -= 1
    return bb, ts


def kernel(x_prompt, x_sample, cache_conv_a, cache_band_k, cache_band_v, cache_conv_c, cache_mem_k, cache_mem_v, mem_prompt, norm_mix, w_in, conv_a_w, conv_a_b, ln_a_g, ln_a_b, q_norm, k_norm, rel_bias, conv_c_w, out_norm, w_out, norm_x, norm_mem, w_xq, w_xk, w_xv, xq_norm, xk_norm, w_xo, norm_ff, w_up, w_down):
    depth = w_in.shape[0]
    bp, seq_p, d = x_prompt.shape
    bs, seq_s, _ = x_sample.shape
    d_conv = conv_a_w.shape[-1]
    d_sc = conv_c_w.shape[-1]
    d_att = N_HEADS_B * HEAD_DIM_B
    past = cache_band_k.shape[2]
    assert past == BAND_PAST and seq_s == CHUNK, "sample step: one chunk behind a full band"

    bb_p, ts_p = _tile_config(bp, seq_p)
    bb_s, ts_s = _tile_config(bs, seq_s)
    ts_mix_p = 2 * ts_p if bb_p == 1 and seq_p % (2 * ts_p) == 0 else ts_p
    tq_p = min(MAX_TQ, seq_p)
    sub_p = max(s for s in (1, 2, 4) if seq_p % (s * tq_p) == 0)
    bb_band_s = 4 if bs % 4 == 0 else 1

    row = lambda a: a[:, None, :]
    w_in_b, w_out_b, w_xq_b, w_xk_b, w_xv_b, w_xo_b, w_up_b, w_down_b = (
        a.astype(BF16) for a in (w_in, w_out, w_xq, w_xk, w_xv, w_xo, w_up, w_down))
    nmem, xkn = row(norm_mem), row(xk_norm)
    onb = row(out_norm[:, d_conv:d_conv + d_att])
    mix_w = (row(norm_mix), w_in_b, conv_a_w, row(conv_a_b), row(ln_a_g), row(ln_a_b),
             row(jnp.tile(q_norm, (1, N_HEADS_B))), row(jnp.tile(k_norm, (1, N_HEADS_B))),
             conv_c_w, row(out_norm[:, :d_conv]), row(out_norm[:, d_conv + d_att:]))
    post_w = (w_out_b, row(norm_x), w_xq_b, row(xq_norm), w_xo_b, row(norm_ff), w_up_b, w_down_b)
    head_id = jnp.arange(d_att) // HEAD_DIM_B
    hsum = (head_id[:, None] == head_id[None, :]).astype(BF16)

    bias_gen = _bias_generator(rel_bias)

    mk_p, mv_p, mk_ph, mv_ph = _memory_kv(mem_prompt, nmem, w_xk_b, w_xv_b, xkn)
    mk_sh = jnp.transpose(cache_mem_k, (0, 1, 3, 2, 4)).astype(BF16)
    mv_sh = jnp.transpose(cache_mem_v, (0, 1, 3, 2, 4)).astype(BF16)
    kct = jnp.transpose(cache_band_k, (0, 1, 3, 4, 2))
    vct = jnp.transpose(cache_band_v, (0, 1, 3, 4, 2))

    zero_a = jnp.zeros((bp, CONV_A_WIDTH - 1, d_conv), F32)
    zero_c = jnp.zeros((bp, CONV_C_WIDTH - 1, d_sc), F32)

    yp, ys = x_prompt, x_sample
    ca_p, kb_p, vb_p, cc_p = [], [], [], []
    ca_s, kb_s, vb_s, cc_s = [], [], [], []
    for l in range(depth):
        ya, yc, q, k, v, sa, sc, kf, vf = _mixer_in(
            yp, zero_a, zero_c, None, mix_w, hsum, l, bb=bb_p, ts=ts_mix_p, name=f"mixer_in_p{l}")
        yb = _band_prompt(q, k, v, bias_gen, onb, l, tq=tq_p, sub=sub_p, name=f"band_p{l}")
        yp = _post(yp, ya, yb, yc, mk_ph, mv_ph, post_w, l, bb=bb_p, ts=ts_p, name=f"post_p{l}")
        ca_p.append(sa); cc_p.append(sc); kb_p.append(kf); vb_p.append(vf)

        ya, yc, q, k, v, sa, sc, kf, vf = _mixer_in(
            ys, cache_conv_a, cache_conv_c, l, mix_w, hsum, l, bb=bb_s, ts=ts_s,
            name=f"mixer_in_s{l}")
        yb = _band_sample(q, k, v, kct, vct, bias_gen, onb, l, bb=bb_band_s, name=f"band_s{l}")
        ys = _post(ys, ya, yb, yc, mk_sh, mv_sh, post_w, l, bb=bb_s, ts=ts_s, name=f"post_s{l}")
        ca_s.append(sa); cc_s.append(sc); kb_s.append(kf); vb_s.append(vf)

    knew, vnew = _band_cache_update(kct, vct, jnp.stack(kb_s), jnp.stack(vb_s), bb=bb_band_s)

    def rows_major(cache_t):
        return jnp.transpose(cache_t, (0, 1, 4, 2, 3))

    def split_heads(parts, nb):
        return jnp.stack(parts).reshape(depth, nb, -1, N_HEADS_B, HEAD_DIM_B)

    return (yp, ys,
            jnp.stack(ca_p), split_heads(kb_p, bp), split_heads(vb_p, bp), jnp.stack(cc_p),
            mk_p.reshape(depth, bp, -1, N_HEADS_X, HEAD_DIM_X),
            mv_p.reshape(depth, bp, -1, N_HEADS_X, HEAD_DIM_X),
            jnp.stack(ca_s), rows_major(knew), rows_major(vnew), jnp.stack(cc_s))
```
